```python
import jax, jax.numpy as jnp
from jax import lax
import numpy as np

D_MODEL = 1024
BATCH = 8
SEQ = 2048
DEPTH = 4

HEAD_DIM = 64
N_HEADS_TOTAL = D_MODEL // HEAD_DIM
N_HEADS_M = 4
N_HEADS_A = (N_HEADS_TOTAL - N_HEADS_M) // 2
N_HEADS_B = N_HEADS_TOTAL - N_HEADS_M - N_HEADS_A
WIDTH_A = N_HEADS_A * HEAD_DIM
WIDTH_B = N_HEADS_B * HEAD_DIM
WIDTH_M = N_HEADS_M * HEAD_DIM
MIX_WIDTH = WIDTH_A + WIDTH_B + WIDTH_M
IDX_HEADS = 4
IDX_DIM = 64
ROT_DIM = HEAD_DIM // 4
ROPE_THETA = 500000.0
CHUNK = 64
PREV_CHUNKS = 8
BAND_CHUNKS = PREV_CHUNKS + 1
REL_CLIP = 256
N_MEM = 256
TOPK_MAX = 256
Q_BLOCK = 128
D_FF = -(-8 * D_MODEL // (3 * 256)) * 256
IN_SIZES = (WIDTH_A, WIDTH_A, WIDTH_A,
            IDX_HEADS * IDX_DIM, IDX_DIM, IDX_HEADS,
            WIDTH_B, WIDTH_B, WIDTH_B,
            WIDTH_M)
D_IN = sum(IN_SIZES)
EPS = 1e-6

kernel_name = "hybrid_dsa_chunkband_memory_block"


def rms_norm(x, g):
    xf = x.astype(jnp.float32)
    y = xf * lax.rsqrt(jnp.mean(xf * xf, axis=-1, keepdims=True) + EPS)
    return (y * g.astype(jnp.float32)).astype(x.dtype)


def partial_rope(x, positions):
    half = ROT_DIM // 2
    inv_freq = jnp.power(ROPE_THETA, -jnp.arange(half, dtype=jnp.float32) / half)
    ang = positions.astype(jnp.float32)[..., None] * inv_freq
    cos = jnp.cos(ang)[:, :, None, :]
    sin = jnp.sin(ang)[:, :, None, :]
    xf = x.astype(jnp.float32)
    x1 = xf[..., :half]
    x2 = xf[..., half:ROT_DIM]
    out = jnp.concatenate([x1 * cos - x2 * sin, x2 * cos + x1 * sin, xf[..., ROT_DIM:]], axis=-1)
    return out.astype(x.dtype)


def dsa_sparse_attention(q, k, v, q_idx, k_idx, w_idx):
    B, S = q.shape[0], q.shape[1]
    topk = min(TOPK_MAX, S // 4)
    nqb = S // Q_BLOCK
    key_chunk = jnp.arange(S) // CHUNK
    k_idx_f = k_idx.astype(jnp.float32)
    starts = jnp.arange(nqb) * Q_BLOCK

    def to_blocks(t):
        return t.reshape((B, nqb, Q_BLOCK) + t.shape[2:]).swapaxes(0, 1)

    def one_block(args):
        qb, qib, wb, start = args
        q_chunk = (start + jnp.arange(Q_BLOCK)) // CHUNK
        allowed = key_chunk[None, :] <= q_chunk[:, None]
        dots = jnp.einsum('bqhd,bsd->bqhs', qib.astype(jnp.float32), k_idx_f) * (IDX_DIM ** -0.5)
        score = jnp.einsum('bqh,bqhs->bqs', wb.astype(jnp.float32) * (IDX_HEADS ** -0.5), jax.nn.relu(dots))
        score = jnp.where(allowed[None], score, -jnp.inf)
        _, sel = lax.top_k(score, topk)
        valid = key_chunk[sel] <= q_chunk[None, :, None]
        k_sel = jax.vmap(lambda kk, ii: kk[ii])(k, sel)
        v_sel = jax.vmap(lambda vv, ii: vv[ii])(v, sel)
        logits = jnp.einsum('bqhd,bqkhd->bhqk', qb, k_sel).astype(jnp.float32) * (HEAD_DIM ** -0.5)
        logits = jnp.where(valid[:, None], logits, -jnp.inf)
        p = jax.nn.softmax(logits, axis=-1).astype(v.dtype)
        return jnp.einsum('bhqk,bqkhd->bqhd', p, v_sel)

    out = lax.map(one_block, (to_blocks(q), to_blocks(q_idx), to_blocks(w_idx), starts))
    return out.swapaxes(0, 1).reshape(B, S, -1)


def chunked_relbias_attention(q, k, v, rel_bias):
    B, S, H, Dh = q.shape
    nc = S // CHUNK
    qc = q.reshape(B, nc, CHUNK, H, Dh)

    def band(t):
        tc = t.reshape(B, nc, CHUNK, H, Dh)
        tp = jnp.pad(tc, ((0, 0), (PREV_CHUNKS, 0), (0, 0), (0, 0), (0, 0)))
        return jnp.concatenate([tp[:, j:j + nc] for j in range(BAND_CHUNKS)], axis=2)

    kb, vb = band(k), band(v)
    i = jnp.arange(CHUNK)
    m = jnp.arange(BAND_CHUNKS * CHUNK)
    rel = PREV_CHUNKS * CHUNK + i[:, None] - m[None, :]
    bias = rel_bias[:, jnp.clip(rel, -REL_CLIP, REL_CLIP) + REL_CLIP].astype(jnp.float32)
    key_chunk = jnp.arange(nc)[:, None] - PREV_CHUNKS + m[None, :] // CHUNK
    valid = key_chunk >= 0
    logits = jnp.einsum('bnqhd,bnkhd->bnhqk', qc, kb).astype(jnp.float32) * (HEAD_DIM ** -0.5) + bias[None, None]
    logits = jnp.where(valid[None, :, None, None, :], logits, -jnp.inf)
    p = jax.nn.softmax(logits, axis=-1).astype(v.dtype)
    out = jnp.einsum('bnhqk,bnkhd->bnqhd', p, vb)
    return out.reshape(B, S, H * Dh)


def memory_cross_attention(q, mem_k, mem_v):
    B, S = q.shape[0], q.shape[1]
    logits = jnp.einsum('bshd,bnhd->bhsn', q, mem_k).astype(jnp.float32) * (HEAD_DIM ** -0.5)
    p = jax.nn.softmax(logits, axis=-1).astype(mem_v.dtype)
    return jnp.einsum('bhsn,bnhd->bshd', p, mem_v).reshape(B, S, -1)


def setup_inputs(seed: int = 0) -> dict:
    key = jax.random.key(seed)
    ks = jax.random.split(key, 20)
    f32 = jnp.float32

    def nrm(k, shape, scale):
        return jax.random.normal(k, shape, f32) * scale

    def gain(k, shape):
        return 1.0 + 0.02 * jax.random.normal(k, shape, f32)

    x = jax.random.normal(ks[0], (BATCH, SEQ, D_MODEL), f32)
    mem = jax.random.normal(ks[1], (BATCH, N_MEM, D_MODEL), f32)
    offsets = jax.random.randint(ks[2], (BATCH, 1), 0, 64) * CHUNK
    positions = (offsets + jnp.arange(SEQ)[None, :]).astype(jnp.int32)
    return {
        "x": x,
        "mem": mem,
        "positions": positions,
        "g_mix": gain(ks[3], (DEPTH, D_MODEL)),
        "w_in": nrm(ks[4], (DEPTH, D_MODEL, D_IN), D_MODEL ** -0.5),
        "g_q_a": gain(ks[5], (DEPTH, HEAD_DIM)),
        "g_k_a": gain(ks[6], (DEPTH, HEAD_DIM)),
        "g_k_idx": gain(ks[7], (DEPTH, IDX_DIM)),
        "g_q_b": gain(ks[8], (DEPTH, HEAD_DIM)),
        "g_k_b": gain(ks[9], (DEPTH, HEAD_DIM)),
        "rel_bias": nrm(ks[10], (DEPTH, N_HEADS_B, 2 * REL_CLIP + 1), 0.1),
        "g_q_m": gain(ks[11], (DEPTH, HEAD_DIM)),
        "g_k_m": gain(ks[12], (DEPTH, HEAD_DIM)),
        "g_mem": gain(ks[13], (DEPTH, D_MODEL)),
        "w_mem_kv": nrm(ks[14], (DEPTH, D_MODEL, 2 * WIDTH_M), D_MODEL ** -0.5),
        "w_out": nrm(ks[15], (DEPTH, MIX_WIDTH, D_MODEL), 0.5 * MIX_WIDTH ** -0.5),
        "g_ffn": gain(ks[16], (DEPTH, D_MODEL)),
        "w_gate_up": nrm(ks[17], (DEPTH, D_MODEL, 2 * D_FF), D_MODEL ** -0.5),
        "w_down": nrm(ks[18], (DEPTH, D_FF, D_MODEL), 0.5 * D_FF ** -0.5),
    }


def reference(x, mem, positions, g_mix, w_in, g_q_a, g_k_a, g_k_idx, g_q_b, g_k_b, rel_bias,
              g_q_m, g_k_m, g_mem, w_mem_kv, w_out, g_ffn, w_gate_up, w_down):
    B, S = x.shape[0], x.shape[1]
    n_mem = mem.shape[1]
    split_points = np.cumsum(IN_SIZES)[:-1].tolist()

    def heads(t, n):
        return t.reshape(B, S, n, HEAD_DIM)

    for l in range(DEPTH):
        h = rms_norm(x, g_mix[l])
        proj = h @ w_in[l]
        qa, ka, va, qi, ki, wi, qb, kb, vb, qm = jnp.split(proj, split_points, axis=-1)

        qa = partial_rope(rms_norm(heads(qa, N_HEADS_A), g_q_a[l]), positions)
        ka = partial_rope(rms_norm(heads(ka, N_HEADS_A), g_k_a[l]), positions)
        va = heads(va, N_HEADS_A)
        qi = partial_rope(qi.reshape(B, S, IDX_HEADS, IDX_DIM), positions)
        ki = partial_rope(rms_norm(ki, g_k_idx[l])[:, :, None, :], positions)[:, :, 0]
        out_a = dsa_sparse_attention(qa, ka, va, qi, ki, wi)

        qb = rms_norm(heads(qb, N_HEADS_B), g_q_b[l])
        kb = rms_norm(heads(kb, N_HEADS_B), g_k_b[l])
        vb = heads(vb, N_HEADS_B)
        out_b = chunked_relbias_attention(qb, kb, vb, rel_bias[l])

        qm = rms_norm(heads(qm, N_HEADS_M), g_q_m[l])
        mkv = rms_norm(mem, g_mem[l]) @ w_mem_kv[l]
        mk, mv = jnp.split(mkv, 2, axis=-1)
        mk = rms_norm(mk.reshape(B, n_mem, N_HEADS_M, HEAD_DIM), g_k_m[l])
        mv = mv.reshape(B, n_mem, N_HEADS_M, HEAD_DIM)
        out_m = memory_cross_attention(qm, mk, mv)

        x = x + jnp.concatenate([out_a, out_b, out_m], axis=-1) @ w_out[l]

        h = rms_norm(x, g_ffn[l])
        gate, up = jnp.split(h @ w_gate_up[l], 2, axis=-1)
        x = x + (jax.nn.silu(gate) * up) @ w_down[l]
    return x
```

```python
import functools

import jax
import jax.numpy as jnp
from jax import lax
from jax.experimental import pallas as pl
from jax.experimental.pallas import tpu as pltpu

F32 = jnp.float32
BF16 = jnp.bfloat16
I32 = jnp.int32

D_MODEL = 1024
HEAD_DIM = 64
N_HEADS_A = 6
N_HEADS_B = 6
N_HEADS_M = 4
WIDTH_A = N_HEADS_A * HEAD_DIM
WIDTH_B = N_HEADS_B * HEAD_DIM
WIDTH_M = N_HEADS_M * HEAD_DIM
IDX_HEADS = 4
IDX_DIM = 64
ROT_DIM = HEAD_DIM // 4
ROT_HALF = ROT_DIM // 2
ROPE_THETA = 500000.0
CHUNK = 64
PREV_CHUNKS = 8
REL_CLIP = 256
TOPK_MAX = 256
D_FF = 2816
EPS = 1e-6

LANES = 128
SUBLANES = 8
PAIR = 2 * HEAD_DIM
TM = 512
TQ = 256
KB = 256
BAND_BLOCKS = 3
WI_ROWS = 8
INT_MIN = -(2 ** 31)
NEG_INF = float("-inf")
VMEM_LIMIT = 56 * 1024 * 1024

R_QA = 0
R_KA = R_QA + WIDTH_A
R_VA = R_KA + WIDTH_A
R_QI = R_VA + WIDTH_A
R_KI = R_QI + IDX_HEADS * IDX_DIM
R_WI = R_KI + IDX_DIM
R_QB = R_WI + WI_ROWS
R_KB = R_QB + WIDTH_B
R_VB = R_KB + WIDTH_B
R_QM = R_VB + WIDTH_B
R_END = R_QM + WIDTH_M

NT_DIMS = (((1,), (1,)), ((), ()))


def _dot(a, b):
    return jnp.dot(a, b, preferred_element_type=F32)


def _dot_nt(a, b):
    return lax.dot_general(a, b, NT_DIMS, preferred_element_type=F32)


def _chunk_of(pos):
    return jnp.right_shift(pos, CHUNK.bit_length() - 1)


def _row_rms(x, g):
    ms = jnp.mean(x * x, axis=-1, keepdims=True)
    return x * lax.rsqrt(ms + EPS) * g


def _head_norm_t(blk, g_col):
    ms = jnp.mean(blk * blk, axis=0, keepdims=True)
    return blk * lax.rsqrt(ms + EPS) * g_col


def _rope_t(y, cos, sin):
    y1 = y[0:ROT_HALF]
    y2 = y[ROT_HALF:ROT_DIM]
    return jnp.concatenate(
        [y1 * cos - y2 * sin, y2 * cos + y1 * sin, y[ROT_DIM:]], axis=0)


def _pair_slot(y, odd):
    z = jnp.zeros_like(y)
    return jnp.concatenate([z, y] if odd else [y, z], axis=0)


def _proj_kernel(x_ref, pos_ref, invf_ref, gmix_ref, wt_ref, gains_ref,
                 qa_ref, ka_ref, va_ref, qi_ref, ki_ref, wi_ref,
                 qb_ref, kb_ref, vb_ref, qm_ref):
    h = _row_rms(x_ref[0], gmix_ref[...]).astype(BF16)
    ang = invf_ref[...] * pos_ref[0].astype(F32)
    cos = jnp.cos(ang)
    sin = jnp.sin(ang)
    q_scale = HEAD_DIM ** -0.5
    n_blk = TM // KB

    def proj_t(r0, r1):
        return _dot_nt(wt_ref[r0:r1, :], h)

    def gain(i):
        return gains_ref[i]

    def head(t, i):
        return t[i * HEAD_DIM:(i + 1) * HEAD_DIM]

    def store_q(ref, t, n_heads, g, rope):
        for i in range(n_heads):
            y = head(t, i)
            if g is not None:
                y = _head_norm_t(y, g)
            if rope:
                y = _rope_t(y, cos, sin)
            ref[0, i * PAIR:(i + 1) * PAIR, :] = _pair_slot(
                y * q_scale, i % 2 == 1).astype(BF16)

    def store_k(ref, t, n_heads, g, rope):
        for p in range(n_heads // 2):
            ys = []
            for i in (2 * p, 2 * p + 1):
                y = _head_norm_t(head(t, i), g)
                ys.append(_rope_t(y, cos, sin) if rope else y)
            blk = jnp.concatenate(ys, axis=0).T.astype(BF16)
            for j in range(n_blk):
                ref[0, j, :, p * PAIR:(p + 1) * PAIR] = blk[j * KB:(j + 1) * KB]

    def store_v(ref, t):
        tb = t.astype(BF16)
        for j in range(n_blk):
            ref[0, j] = tb[:, j * KB:(j + 1) * KB]

    store_q(qa_ref, proj_t(R_QA, R_KA), N_HEADS_A, gain(0), True)
    store_k(ka_ref, proj_t(R_KA, R_VA), N_HEADS_A, gain(1), True)
    store_v(va_ref, proj_t(R_VA, R_QI))

    t = proj_t(R_QI, R_QB)
    for i in range(IDX_HEADS):
        y = _rope_t(head(t, i), cos, sin) * (IDX_DIM ** -0.5)
        qi_ref[0, i * PAIR:(i + 1) * PAIR, :] = _pair_slot(y, False).astype(BF16)
    ki = _rope_t(_head_norm_t(head(t, IDX_HEADS), gain(2)), cos, sin)
    ki_blk = _pair_slot(ki, False).T.astype(BF16)
    for j in range(n_blk):
        ki_ref[0, j] = ki_blk[j * KB:(j + 1) * KB]
    wi_ref[0] = t[R_WI - R_QI:R_QB - R_QI] * (IDX_HEADS ** -0.5)

    store_q(qb_ref, proj_t(R_QB, R_KB), N_HEADS_B, gain(3), False)
    store_k(kb_ref, proj_t(R_KB, R_VB), N_HEADS_B, gain(4), False)
    store_v(vb_ref, proj_t(R_VB, R_QM))
    store_q(qm_ref, proj_t(R_QM, R_END), N_HEADS_M, gain(5), False)


def _proj_call(x, pos3, invf, gmix, wt, gains):
    B, S, _ = x.shape
    nkb = S // KB
    n_blk = TM // KB

    def tok_t(rows):
        return pl.BlockSpec((1, rows, TM), lambda b, i: (b, 0, i))

    def full(shape):
        return pl.BlockSpec(shape, lambda b, i: (0,) * len(shape))

    def krow(width):
        return pl.BlockSpec((1, n_blk, KB, width), lambda b, i: (b, i, 0, 0))

    def vt(rows):
        return pl.BlockSpec((1, n_blk, rows, KB), lambda b, i: (b, i, 0, 0))

    out_shape = (
        jax.ShapeDtypeStruct((B, N_HEADS_A * PAIR, S), BF16),
        jax.ShapeDtypeStruct((B, nkb, KB, WIDTH_A), BF16),
        jax.ShapeDtypeStruct((B, nkb, WIDTH_A, KB), BF16),
        jax.ShapeDtypeStruct((B, IDX_HEADS * PAIR, S), BF16),
        jax.ShapeDtypeStruct((B, nkb, KB, PAIR), BF16),
        jax.ShapeDtypeStruct((B, WI_ROWS, S), F32),
        jax.ShapeDtypeStruct((B, N_HEADS_B * PAIR, S), BF16),
        jax.ShapeDtypeStruct((B, nkb, KB, WIDTH_B), BF16),
        jax.ShapeDtypeStruct((B, nkb, WIDTH_B, KB), BF16),
        jax.ShapeDtypeStruct((B, N_HEADS_M * PAIR, S), BF16),
    )
    out_specs = (
        tok_t(N_HEADS_A * PAIR), krow(WIDTH_A), vt(WIDTH_A),
        tok_t(IDX_HEADS * PAIR), krow(PAIR), tok_t(WI_ROWS),
        tok_t(N_HEADS_B * PAIR), krow(WIDTH_B), vt(WIDTH_B),
        tok_t(N_HEADS_M * PAIR),
    )
    return pl.pallas_call(
        _proj_kernel,
        grid=(B, S // TM),
        in_specs=[
            pl.BlockSpec((1, TM, D_MODEL), lambda b, i: (b, i, 0)),
            pl.BlockSpec((1, 1, TM), lambda b, i: (b, 0, i)),
            full((ROT_HALF, 1)),
            full((1, D_MODEL)),
            full((R_END, D_MODEL)),
            full((6, HEAD_DIM, 1)),
        ],
        out_specs=out_specs,
        out_shape=out_shape,
        compiler_params=pltpu.CompilerParams(
            dimension_semantics=("parallel", "parallel"),
            vmem_limit_bytes=VMEM_LIMIT),
        name="in_proj",
    )(x, pos3, invf, gmix, wt, gains)


def _memkv_kernel(mem_ref, gmem_ref, wt_ref, gk_ref, mk_ref, mv_ref):
    hm = _row_rms(mem_ref[0], gmem_ref[...]).astype(BF16)
    kt = _dot_nt(wt_ref[0:WIDTH_M, :], hm)
    for p in range(N_HEADS_M // 2):
        ys = [_head_norm_t(kt[i * HEAD_DIM:(i + 1) * HEAD_DIM], gk_ref[...])
              for i in (2 * p, 2 * p + 1)]
        mk_ref[0, :, p * PAIR:(p + 1) * PAIR] = (
            jnp.concatenate(ys, axis=0).T.astype(BF16))
    mv_ref[0] = _dot_nt(wt_ref[WIDTH_M:2 * WIDTH_M, :], hm).astype(BF16)


def _memkv_call(mem, gmem, wt, gk):
    B, n_mem, _ = mem.shape
    return pl.pallas_call(
        _memkv_kernel,
        grid=(B,),
        in_specs=[
            pl.BlockSpec((1, n_mem, D_MODEL), lambda b: (b, 0, 0)),
            pl.BlockSpec((1, D_MODEL), lambda b: (0, 0)),
            pl.BlockSpec((2 * WIDTH_M, D_MODEL), lambda b: (0, 0)),
            pl.BlockSpec((HEAD_DIM, 1), lambda b: (0, 0)),
        ],
        out_specs=(
            pl.BlockSpec((1, n_mem, WIDTH_M), lambda b: (b, 0, 0)),
            pl.BlockSpec((1, WIDTH_M, n_mem), lambda b: (b, 0, 0)),
        ),
        out_shape=(
            jax.ShapeDtypeStruct((B, n_mem, WIDTH_M), BF16),
            jax.ShapeDtypeStruct((B, WIDTH_M, n_mem), BF16),
        ),
        compiler_params=pltpu.CompilerParams(
            dimension_semantics=("parallel",), vmem_limit_bytes=VMEM_LIMIT),
        name="mem_kv",
    )(mem, gmem, wt, gk)


def _bias_kernel(f_ref, out_ref):
    n_keys = BAND_BLOCKS * KB
    width = f_ref.shape[-1]
    base = jnp.broadcast_to(f_ref[0, 0], (n_keys, width))
    rolled = pltpu.roll(base, width - (n_keys - 1), 1, stride=1, stride_axis=0)
    tab = rolled[:, 0:TQ]
    key_c = _chunk_of(lax.broadcasted_iota(I32, (n_keys, TQ), 0))
    qry_c = _chunk_of(lax.broadcasted_iota(I32, (n_keys, TQ), 1)
                      + (BAND_BLOCKS - 1) * KB)
    ok = (key_c <= qry_c) & (key_c >= qry_c - PREV_CHUNKS)
    out_ref[0, 0] = jnp.where(ok, tab, NEG_INF)


def _bias_call(fvec):
    depth, n_heads, _, width = fvec.shape
    n_keys = BAND_BLOCKS * KB
    return pl.pallas_call(
        _bias_kernel,
        grid=(depth, n_heads),
        in_specs=[pl.BlockSpec((1, 1, 1, width), lambda l, h: (l, h, 0, 0))],
        out_specs=pl.BlockSpec((1, 1, n_keys, TQ), lambda l, h: (l, h, 0, 0)),
        out_shape=jax.ShapeDtypeStruct((depth, n_heads, n_keys, TQ), F32),
        compiler_params=pltpu.CompilerParams(
            dimension_semantics=("parallel", "parallel")),
        name="band_bias",
    )(fvec)


def _col_max(x):
    return jnp.max(x, axis=0, keepdims=True)


def _col_sum(x):
    return jnp.sum(x, axis=0, keepdims=True)


def _finish_pair(outs):
    return jnp.concatenate(outs, axis=0).T.astype(BF16)


def _dsa_kernel(qa_ref, ka_ref, va_ref, qi_ref, ki_ref, wi_ref, out_ref,
                keys_ref, mask_ref, lg_ref):
    t = pl.program_id(1)
    nkb = t + 1
    n_grp = KB // SUBLANES
    qry_c = _chunk_of(t * TQ + lax.broadcasted_iota(I32, (KB, TQ), 1))
    key_c0 = _chunk_of(lax.broadcasted_iota(I32, (KB, TQ), 0))

    def score_blk(kb, carry):
        kblk = ki_ref[0, kb]
        s = jnp.zeros((KB, TQ), F32)
        for i in range(IDX_HEADS):
            d = _dot(kblk, qi_ref[0, i * PAIR:(i + 1) * PAIR, :])
            s = s + jnp.maximum(d, 0.0) * wi_ref[0, i:i + 1, :]
        s = jnp.where(s == 0.0, 0.0, s)
        bits = lax.bitcast_convert_type(s, I32)
        key = jnp.where(bits >= 0, bits, bits ^ jnp.int32(0x7FFFFFFF))
        allowed = (key_c0 + kb * (KB // CHUNK)) <= qry_c
        keys_ref[kb] = jnp.where(allowed, key, jnp.int32(INT_MIN))
        return carry

    lax.fori_loop(0, nkb, score_blk, 0)

    def count(cmp, thr):
        thr8 = jnp.broadcast_to(thr, (SUBLANES, TQ))

        def blk(kb, acc):
            for g in range(n_grp):
                k = keys_ref[kb, g * SUBLANES:(g + 1) * SUBLANES, :]
                acc = acc + jnp.where(cmp(k, thr8), 1, 0)
            return acc

        acc = lax.fori_loop(0, nkb, blk, jnp.zeros((SUBLANES, TQ), I32))
        return _col_sum(acc)

    ge = lambda a, b: a >= b
    gt = lambda a, b: a > b

    zero = jnp.zeros((1, TQ), I32)
    thr = jnp.where(count(ge, zero) >= TOPK_MAX, zero, jnp.int32(INT_MIN))

    def bit_body(i, thr):
        cand = thr | lax.shift_left(jnp.int32(1), 30 - i)
        return jnp.where(count(ge, cand) >= TOPK_MAX, cand, thr)

    thr = lax.fori_loop(0, 31, bit_body, thr)
    n_gt = count(gt, thr)
    n_ge = count(ge, thr)
    need = (TOPK_MAX - n_gt).astype(F32)
    tie = (n_ge > TOPK_MAX) & (thr > INT_MIN)
    any_tie = jnp.max(jnp.where(tie, 1.0, 0.0)) > 0.0

    @pl.when(jnp.logical_not(any_tie))
    def _():
        floor = jnp.maximum(thr, jnp.int32(INT_MIN + 1))

        def blk(kb, carry):
            mask_ref[kb] = jnp.where(keys_ref[kb] >= floor, 0.0, NEG_INF)
            return carry

        lax.fori_loop(0, nkb, blk, 0)

    @pl.when(any_tie)
    def _():
        lower = (lax.broadcasted_iota(I32, (KB, KB), 1)
                 < lax.broadcasted_iota(I32, (KB, KB), 0))
        ltri = jnp.where(lower, 1.0, 0.0).astype(BF16)

        def blk(kb, seen):
            k = keys_ref[kb]
            eq = jnp.where(k == thr, 1.0, 0.0)
            rank = _dot(ltri, eq.astype(BF16)) + seen
            keep = (k > thr) | ((k == thr) & (rank < need))
            keep = keep & (k > INT_MIN)
            mask_ref[kb] = jnp.where(keep, 0.0, NEG_INF)
            return seen + _col_sum(eq)

        lax.fori_loop(0, nkb, blk, jnp.zeros((1, TQ), F32))

    for p in range(N_HEADS_A // 2):
        outs = []
        for hh in range(2):
            q = qa_ref[0, (2 * p + hh) * PAIR:(2 * p + hh + 1) * PAIR, :]

            def pass1(kb, m, q=q, p=p):
                lg = _dot(ka_ref[0, kb, :, p * PAIR:(p + 1) * PAIR], q)
                lg = lg + mask_ref[kb]
                lg_ref[kb] = lg
                return jnp.maximum(m, _col_max(lg))

            m = lax.fori_loop(0, nkb, pass1, jnp.full((1, TQ), NEG_INF, F32))

            def pass2(kb, carry, m=m, p=p):
                l, acc = carry
                e = jnp.exp(lg_ref[kb] - m)
                acc = acc + _dot(va_ref[0, kb, p * PAIR:(p + 1) * PAIR, :],
                                 e.astype(BF16))
                return l + _col_sum(e), acc

            l, acc = lax.fori_loop(
                0, nkb, pass2,
                (jnp.zeros((1, TQ), F32), jnp.zeros((PAIR, TQ), F32)))
            outs.append(acc[hh * HEAD_DIM:(hh + 1) * HEAD_DIM] / l)
        out_ref[0, :, p * PAIR:(p + 1) * PAIR] = _finish_pair(outs)


def _dsa_call(qa, ka, va, qi, ki, wi):
    B, nkb = ka.shape[0], ka.shape[1]
    S = nkb * KB

    def tile_t(rows):
        return pl.BlockSpec((1, rows, TQ), lambda b, t: (b, 0, t))

    def whole(a):
        return pl.BlockSpec((1,) + a.shape[1:], lambda b, t: (b, 0, 0, 0))

    return pl.pallas_call(
        _dsa_kernel,
        grid=(B, S // TQ),
        in_specs=[tile_t(N_HEADS_A * PAIR), whole(ka), whole(va),
                  tile_t(IDX_HEADS * PAIR), whole(ki), tile_t(WI_ROWS)],
        out_specs=pl.BlockSpec((1, TQ, WIDTH_A), lambda b, t: (b, t, 0)),
        out_shape=jax.ShapeDtypeStruct((B, S, WIDTH_A), BF16),
        scratch_shapes=[
            pltpu.VMEM((nkb, KB, TQ), I32),
            pltpu.VMEM((nkb, KB, TQ), F32),
            pltpu.VMEM((nkb, KB, TQ), F32),
        ],
        compiler_params=pltpu.CompilerParams(
            dimension_semantics=("parallel", "parallel"),
            vmem_limit_bytes=VMEM_LIMIT),
        name="dsa_attn",
    )(qa, ka, va, qi, ki, wi)


def _band_mem_kernel(qb_ref, kb_ref, vb_ref, bias_ref, qm_ref, mk_ref, mv_ref,
                     out_ref):
    t = pl.program_id(1)
    for p in range(N_HEADS_B // 2):
        outs = []
        for hh in range(2):
            h = 2 * p + hh
            q = qb_ref[0, h * PAIR:(h + 1) * PAIR, :]
            lgs = []
            for r in range(BAND_BLOCKS):
                blk = t - (BAND_BLOCKS - 1) + r
                src = jnp.maximum(blk, 0)
                lg = _dot(kb_ref[0, src, :, p * PAIR:(p + 1) * PAIR], q)
                lg = lg + bias_ref[0, h, r * KB:(r + 1) * KB, :]
                lgs.append((src, lg + jnp.where(blk >= 0, 0.0, NEG_INF)))
            m = functools.reduce(jnp.maximum, [_col_max(lg) for _, lg in lgs])
            l = jnp.zeros((1, TQ), F32)
            acc = jnp.zeros((PAIR, TQ), F32)
            for src, lg in lgs:
                e = jnp.exp(lg - m)
                l = l + _col_sum(e)
                acc = acc + _dot(vb_ref[0, src, p * PAIR:(p + 1) * PAIR, :],
                                 e.astype(BF16))
            outs.append(acc[hh * HEAD_DIM:(hh + 1) * HEAD_DIM] / l)
        out_ref[0, :, p * PAIR:(p + 1) * PAIR] = _finish_pair(outs)

    for p in range(N_HEADS_M // 2):
        outs = []
        for hh in range(2):
            h = 2 * p + hh
            lg = _dot(mk_ref[0, :, p * PAIR:(p + 1) * PAIR],
                      qm_ref[0, h * PAIR:(h + 1) * PAIR, :])
            e = jnp.exp(lg - _col_max(lg))
            acc = _dot(mv_ref[0, p * PAIR:(p + 1) * PAIR, :], e.astype(BF16))
            outs.append(acc[hh * HEAD_DIM:(hh + 1) * HEAD_DIM] / _col_sum(e))
        out_ref[0, :, WIDTH_B + p * PAIR:WIDTH_B + (p + 1) * PAIR] = (
            _finish_pair(outs))


def _band_mem_call(qb, kb, vb, bias, layer, qm, mk, mv):
    B, nkb = kb.shape[0], kb.shape[1]
    S = nkb * KB
    n_mem = mk.shape[1]

    def tile_t(rows):
        return pl.BlockSpec((1, rows, TQ), lambda b, t: (b, 0, t))

    def whole(a):
        return pl.BlockSpec((1,) + a.shape[1:], lambda b, t: (b, 0, 0, 0))

    return pl.pallas_call(
        _band_mem_kernel,
        grid=(B, S // TQ),
        in_specs=[
            tile_t(N_HEADS_B * PAIR), whole(kb), whole(vb),
            pl.BlockSpec((1,) + bias.shape[1:], lambda b, t: (layer, 0, 0, 0)),
            tile_t(N_HEADS_M * PAIR),
            pl.BlockSpec((1, n_mem, WIDTH_M), lambda b, t: (b, 0, 0)),
            pl.BlockSpec((1, WIDTH_M, n_mem), lambda b, t: (b, 0, 0)),
        ],
        out_specs=pl.BlockSpec((1, TQ, WIDTH_B + WIDTH_M), lambda b, t: (b, t, 0)),
        out_shape=jax.ShapeDtypeStruct((B, S, WIDTH_B + WIDTH_M), BF16),
        compiler_params=pltpu.CompilerParams(
            dimension_semantics=("parallel", "parallel"),
            vmem_limit_bytes=VMEM_LIMIT),
        name="band_mem_attn",
    )(qb, kb, vb, bias, qm, mk, mv)


FF_CHUNK = D_FF // 2


def _out_ffn_kernel(x_ref, oa_ref, obm_ref, wo_ref, gffn_ref, wgu_ref, wd_ref,
                    out_ref):
    attn = jnp.concatenate([oa_ref[...], obm_ref[...]], axis=1)
    x1 = x_ref[...] + _dot(attn, wo_ref[...])
    h = _row_rms(x1, gffn_ref[...]).astype(BF16)
    acc = x1
    for c in range(D_FF // FF_CHUNK):
        c0 = c * FF_CHUNK
        gate = _dot(h, wgu_ref[:, c0:c0 + FF_CHUNK])
        up = _dot(h, wgu_ref[:, D_FF + c0:D_FF + c0 + FF_CHUNK])
        act = gate * (1.0 / (1.0 + jnp.exp(-gate))) * up
        acc = acc + _dot(act.astype(BF16), wd_ref[c0:c0 + FF_CHUNK, :])
    out_ref[...] = acc


def _out_ffn_call(x2, oa2, obm2, wo, gffn, wgu, wd):
    n = x2.shape[0]

    def rows(width):
        return pl.BlockSpec((TM, width), lambda i: (i, 0))

    def resident(a):
        return pl.BlockSpec(a.shape, lambda i: (0, 0),
                            pipeline_mode=pl.Buffered(1))

    return pl.pallas_call(
        _out_ffn_kernel,
        grid=(n // TM,),
        in_specs=[rows(D_MODEL), rows(WIDTH_A), rows(WIDTH_B + WIDTH_M),
                  resident(wo), resident(gffn), resident(wgu), resident(wd)],
        out_specs=rows(D_MODEL),
        out_shape=jax.ShapeDtypeStruct((n, D_MODEL), F32),
        compiler_params=pltpu.CompilerParams(
            dimension_semantics=("parallel",), vmem_limit_bytes=VMEM_LIMIT),
        name="out_ffn",
    )(x2, oa2, obm2, wo, gffn, wgu, wd)


def kernel(x, mem, positions, g_mix, w_in, g_q_a, g_k_a, g_k_idx, g_q_b, g_k_b,
           rel_bias, g_q_m, g_k_m, g_mem, w_mem_kv, w_out, g_ffn, w_gate_up,
           w_down):
    B, S, D = x.shape
    depth = w_in.shape[0]

    w_in_t = jnp.swapaxes(w_in, 1, 2)
    n_wi = R_KI + IDX_DIM + IDX_HEADS
    w_in_t = jnp.concatenate(
        [w_in_t[:, :n_wi],
         jnp.zeros((depth, WI_ROWS - IDX_HEADS, D), w_in.dtype),
         w_in_t[:, n_wi:]], axis=1).astype(BF16)
    w_mem_t = jnp.swapaxes(w_mem_kv, 1, 2).astype(BF16)
    w_out_b = w_out.astype(BF16)
    w_gu_b = w_gate_up.astype(BF16)
    w_down_b = w_down.astype(BF16)
    gains = jnp.stack(
        [g_q_a, g_k_a, g_k_idx, g_q_b, g_k_b, g_q_m], axis=1)[..., None]

    inv_freq = jnp.power(
        ROPE_THETA, -jnp.arange(ROT_HALF, dtype=F32) / ROT_HALF)[:, None]
    pos3 = positions.reshape(B, 1, S)

    n_f = 2 * REL_CLIP
    fvec = jnp.concatenate(
        [rel_bias[..., 1:], jnp.broadcast_to(rel_bias[..., -1:],
                                             rel_bias.shape[:-1] + (n_f,))],
        axis=-1)[:, :, None, :]
    bias = _bias_call(fvec)

    for l in range(depth):
        qa, ka, va, qi, ki, wi, qb, kb, vb, qm = _proj_call(
            x, pos3, inv_freq, g_mix[l][None], w_in_t[l], gains[l])
        mk, mv = _memkv_call(mem, g_mem[l][None], w_mem_t[l], g_k_m[l][:, None])
        oa = _dsa_call(qa, ka, va, qi, ki, wi)
        obm = _band_mem_call(qb, kb, vb, bias, l, qm, mk, mv)
        x = _out_ffn_call(
            x.reshape(B * S, D), oa.reshape(B * S, WIDTH_A),
            obm.reshape(B * S, WIDTH_B + WIDTH_M), w_out_b[l], g_ffn[l][None],
            w_gu_b[l], w_down_b[l]).reshape(B, S, D)
    return x
```

```python
import functools

import jax
import jax.numpy as jnp
from jax import lax
from jax.experimental import pallas as pl
from jax.experimental.pallas import tpu as pltpu

F32 = jnp.float32
BF16 = jnp.bfloat16
I32 = jnp.int32

D_MODEL = 1024
HEAD_DIM = 64
N_HEADS_A = 6
N_HEADS_B = 6
N_HEADS_M = 4
WIDTH_A = N_HEADS_A * HEAD_DIM
WIDTH_B = N_HEADS_B * HEAD_DIM
WIDTH_M = N_HEADS_M * HEAD_DIM
IDX_HEADS = 4
IDX_DIM = 64
ROT_DIM = HEAD_DIM // 4
ROT_HALF = ROT_DIM // 2
ROPE_THETA = 500000.0
CHUNK = 64
PREV_CHUNKS = 8
REL_CLIP = 256
TOPK_MAX = 256
D_FF = 2816
EPS = 1e-6

LANES = 128
SUBLANES = 8
PAIR = 2 * HEAD_DIM
TM = 512
TQ = 256
KB = 256
BAND_BLOCKS = 3
WI_ROWS = 8
WI_PAD = 16
INT_MIN = -(2 ** 31)
NEG_INF = float("-inf")
MASKED = -1e30
VMEM_LIMIT = 56 * 1024 * 1024

R_QA = 0
R_KA = R_QA + WIDTH_A
R_VA = R_KA + WIDTH_A
R_QI = R_VA + WIDTH_A
R_KI = R_QI + IDX_HEADS * IDX_DIM
R_WI = R_KI + IDX_DIM
R_QB = R_WI + WI_PAD
R_KB = R_QB + WIDTH_B
R_VB = R_KB + WIDTH_B
R_QM = R_VB + WIDTH_B
R_END = R_QM + WIDTH_M

NT_DIMS = (((1,), (1,)), ((), ()))


def _dot(a, b):
    return jnp.dot(a, b, preferred_element_type=F32)


def _dot_nt(a, b):
    return lax.dot_general(a, b, NT_DIMS, preferred_element_type=F32)


def _chunk_of(pos):
    return jnp.right_shift(pos, CHUNK.bit_length() - 1)


def _row_rms(x, g):
    ms = jnp.mean(x * x, axis=-1, keepdims=True)
    return x * lax.rsqrt(ms + EPS) * g


def _head_norm_t(blk, g_col):
    ms = jnp.mean(blk * blk, axis=0, keepdims=True)
    return blk * lax.rsqrt(ms + EPS) * g_col


def _rope_t(y, cos, sin):
    y1 = y[0:ROT_HALF]
    y2 = y[ROT_HALF:ROT_DIM]
    return jnp.concatenate(
        [y1 * cos - y2 * sin, y2 * cos + y1 * sin, y[ROT_DIM:]], axis=0)


def _pair_slot(y, odd):
    z = jnp.zeros_like(y)
    return jnp.concatenate([z, y] if odd else [y, z], axis=0)


def _proj_kernel(x_ref, pos_ref, invf_ref, gmix_ref, wt_ref, gains_ref,
                 qa_ref, ka_ref, va_ref, qi_ref, ki_ref, wi_ref,
                 qb_ref, kb_ref, vb_ref, qm_ref):
    h = _row_rms(x_ref[0], gmix_ref[...]).astype(BF16)
    ang = invf_ref[...] * pos_ref[0].astype(F32)
    cos = jnp.cos(ang)
    sin = jnp.sin(ang)
    q_scale = HEAD_DIM ** -0.5
    n_blk = TM // KB

    def proj_t(r0, r1):
        return _dot_nt(wt_ref[r0:r1, :], h)

    def gain(i):
        return gains_ref[i]

    def head(t, i):
        return t[i * HEAD_DIM:(i + 1) * HEAD_DIM]

    def store_q(ref, t, n_heads, g, rope):
        for i in range(n_heads):
            y = head(t, i)
            if g is not None:
                y = _head_norm_t(y, g)
            if rope:
                y = _rope_t(y, cos, sin)
            ref[0, i * PAIR:(i + 1) * PAIR, :] = _pair_slot(
                y * q_scale, i % 2 == 1).astype(BF16)

    def store_k(ref, t, n_heads, g, rope):
        for p in range(n_heads // 2):
            ys = []
            for i in (2 * p, 2 * p + 1):
                y = _head_norm_t(head(t, i), g)
                ys.append(_rope_t(y, cos, sin) if rope else y)
            blk = jnp.concatenate(ys, axis=0).T.astype(BF16)
            for j in range(n_blk):
                ref[0, j, :, p * PAIR:(p + 1) * PAIR] = blk[j * KB:(j + 1) * KB]

    def store_v(ref, t):
        tb = t.astype(BF16)
        for j in range(n_blk):
            ref[0, j] = tb[:, j * KB:(j + 1) * KB]

    store_q(qa_ref, proj_t(R_QA, R_KA), N_HEADS_A, gain(0), True)
    store_k(ka_ref, proj_t(R_KA, R_VA), N_HEADS_A, gain(1), True)
    store_v(va_ref, proj_t(R_VA, R_QI))

    t = proj_t(R_QI, R_QB)
    for i in range(IDX_HEADS):
        y = _rope_t(head(t, i), cos, sin) * (IDX_DIM ** -0.5)
        qi_ref[0, i * PAIR:(i + 1) * PAIR, :] = _pair_slot(y, False).astype(BF16)
    ki = _rope_t(_head_norm_t(head(t, IDX_HEADS), gain(2)), cos, sin)
    ki_blk = _pair_slot(ki, False).T.astype(BF16)
    for j in range(n_blk):
        ki_ref[0, j] = ki_blk[j * KB:(j + 1) * KB]
    wi_ref[0] = t[R_WI - R_QI:R_WI - R_QI + WI_ROWS] * (IDX_HEADS ** -0.5)

    store_q(qb_ref, proj_t(R_QB, R_KB), N_HEADS_B, gain(3), False)
    store_k(kb_ref, proj_t(R_KB, R_VB), N_HEADS_B, gain(4), False)
    store_v(vb_ref, proj_t(R_VB, R_QM))
    store_q(qm_ref, proj_t(R_QM, R_END), N_HEADS_M, gain(5), False)


def _proj_call(x, pos3, invf, gmix, wt, gains):
    B, S, _ = x.shape
    nkb = S // KB
    n_blk = TM // KB

    def tok_t(rows):
        return pl.BlockSpec((1, rows, TM), lambda b, i: (b, 0, i))

    def full(shape):
        return pl.BlockSpec(shape, lambda b, i: (0,) * len(shape))

    def krow(width):
        return pl.BlockSpec((1, n_blk, KB, width), lambda b, i: (b, i, 0, 0))

    def vt(rows):
        return pl.BlockSpec((1, n_blk, rows, KB), lambda b, i: (b, i, 0, 0))

    out_shape = (
        jax.ShapeDtypeStruct((B, N_HEADS_A * PAIR, S), BF16),
        jax.ShapeDtypeStruct((B, nkb, KB, WIDTH_A), BF16),
        jax.ShapeDtypeStruct((B, nkb, WIDTH_A, KB), BF16),
        jax.ShapeDtypeStruct((B, IDX_HEADS * PAIR, S), BF16),
        jax.ShapeDtypeStruct((B, nkb, KB, PAIR), BF16),
        jax.ShapeDtypeStruct((B, WI_ROWS, S), F32),
        jax.ShapeDtypeStruct((B, N_HEADS_B * PAIR, S), BF16),
        jax.ShapeDtypeStruct((B, nkb, KB, WIDTH_B), BF16),
        jax.ShapeDtypeStruct((B, nkb, WIDTH_B, KB), BF16),
        jax.ShapeDtypeStruct((B, N_HEADS_M * PAIR, S), BF16),
    )
    out_specs = (
        tok_t(N_HEADS_A * PAIR), krow(WIDTH_A), vt(WIDTH_A),
        tok_t(IDX_HEADS * PAIR), krow(PAIR), tok_t(WI_ROWS),
        tok_t(N_HEADS_B * PAIR), krow(WIDTH_B), vt(WIDTH_B),
        tok_t(N_HEADS_M * PAIR),
    )
    return pl.pallas_call(
        _proj_kernel,
        grid=(B, S // TM),
        in_specs=[
            pl.BlockSpec((1, TM, D_MODEL), lambda b, i: (b, i, 0)),
            pl.BlockSpec((1, 1, TM), lambda b, i: (b, 0, i)),
            full((ROT_HALF, 1)),
            full((1, D_MODEL)),
            full((R_END, D_MODEL)),
            full((6, HEAD_DIM, 1)),
        ],
        out_specs=out_specs,
        out_shape=out_shape,
        compiler_params=pltpu.CompilerParams(
            dimension_semantics=("parallel", "parallel"),
            vmem_limit_bytes=VMEM_LIMIT),
        name="in_proj",
    )(x, pos3, invf, gmix, wt, gains)


def _memkv_kernel(mem_ref, gmem_ref, wt_ref, gk_ref, mk_ref, mv_ref):
    hm = _row_rms(mem_ref[0], gmem_ref[...]).astype(BF16)
    kt = _dot_nt(wt_ref[0:WIDTH_M, :], hm)
    for p in range(N_HEADS_M // 2):
        ys = [_head_norm_t(kt[i * HEAD_DIM:(i + 1) * HEAD_DIM], gk_ref[...])
              for i in (2 * p, 2 * p + 1)]
        mk_ref[0, :, p * PAIR:(p + 1) * PAIR] = (
            jnp.concatenate(ys, axis=0).T.astype(BF16))
    mv_ref[0] = _dot_nt(wt_ref[WIDTH_M:2 * WIDTH_M, :], hm).astype(BF16)


def _memkv_call(mem, gmem, wt, gk):
    B, n_mem, _ = mem.shape
    return pl.pallas_call(
        _memkv_kernel,
        grid=(B,),
        in_specs=[
            pl.BlockSpec((1, n_mem, D_MODEL), lambda b: (b, 0, 0)),
            pl.BlockSpec((1, D_MODEL), lambda b: (0, 0)),
            pl.BlockSpec((2 * WIDTH_M, D_MODEL), lambda b: (0, 0)),
            pl.BlockSpec((HEAD_DIM, 1), lambda b: (0, 0)),
        ],
        out_specs=(
            pl.BlockSpec((1, n_mem, WIDTH_M), lambda b: (b, 0, 0)),
            pl.BlockSpec((1, WIDTH_M, n_mem), lambda b: (b, 0, 0)),
        ),
        out_shape=(
            jax.ShapeDtypeStruct((B, n_mem, WIDTH_M), BF16),
            jax.ShapeDtypeStruct((B, WIDTH_M, n_mem), BF16),
        ),
        compiler_params=pltpu.CompilerParams(
            dimension_semantics=("parallel",), vmem_limit_bytes=VMEM_LIMIT),
        name="mem_kv",
    )(mem, gmem, wt, gk)


def _bias_kernel(f_ref, out_ref):
    n_keys = BAND_BLOCKS * KB
    width = f_ref.shape[-1]
    base = jnp.broadcast_to(f_ref[0, 0], (n_keys, width))
    rolled = pltpu.roll(base, width - (n_keys - 1), 1, stride=1, stride_axis=0)
    tab = rolled[:, 0:TQ]
    key_c = _chunk_of(lax.broadcasted_iota(I32, (n_keys, TQ), 0))
    qry_c = _chunk_of(lax.broadcasted_iota(I32, (n_keys, TQ), 1)
                      + (BAND_BLOCKS - 1) * KB)
    ok = (key_c <= qry_c) & (key_c >= qry_c - PREV_CHUNKS)
    out_ref[0, 0] = jnp.where(ok, tab, NEG_INF)


def _bias_call(fvec):
    depth, n_heads, _, width = fvec.shape
    n_keys = BAND_BLOCKS * KB
    return pl.pallas_call(
        _bias_kernel,
        grid=(depth, n_heads),
        in_specs=[pl.BlockSpec((1, 1, 1, width), lambda l, h: (l, h, 0, 0))],
        out_specs=pl.BlockSpec((1, 1, n_keys, TQ), lambda l, h: (l, h, 0, 0)),
        out_shape=jax.ShapeDtypeStruct((depth, n_heads, n_keys, TQ), F32),
        compiler_params=pltpu.CompilerParams(
            dimension_semantics=("parallel", "parallel")),
        name="band_bias",
    )(fvec)


def _col_max(x):
    return jnp.max(x, axis=0, keepdims=True)


def _col_sum(x):
    return jnp.sum(x, axis=0, keepdims=True)


def _finish_pair(outs):
    return jnp.concatenate(outs, axis=0).T.astype(BF16)


def _dsa_kernel(qa_ref, ka_ref, va_ref, qi_ref, ki_ref, wi_ref, out_ref,
                keys_ref, mask_ref, acc_ref):
    t = pl.program_id(1)
    nkb = t + 1
    n_grp = KB // SUBLANES
    qry_c = _chunk_of(t * TQ + lax.broadcasted_iota(I32, (KB, TQ), 1))
    key_c0 = _chunk_of(lax.broadcasted_iota(I32, (KB, TQ), 0))

    def score_blk(kb, carry):
        kblk = ki_ref[0, kb]
        s = jnp.zeros((KB, TQ), F32)
        for i in range(IDX_HEADS):
            d = _dot(kblk, qi_ref[0, i * PAIR:(i + 1) * PAIR, :])
            s = s + jnp.maximum(d, 0.0) * wi_ref[0, i:i + 1, :]
        s = jnp.where(s == 0.0, 0.0, s)
        bits = lax.bitcast_convert_type(s, I32)
        key = jnp.where(bits >= 0, bits, bits ^ jnp.int32(0x7FFFFFFF))
        allowed = (key_c0 + kb * (KB // CHUNK)) <= qry_c
        keys_ref[kb] = jnp.where(allowed, key, jnp.int32(INT_MIN))
        return carry

    lax.fori_loop(0, nkb, score_blk, 0)

    def count(cmp, thr):
        thr8 = jnp.broadcast_to(thr, (SUBLANES, TQ))

        def blk(kb, acc):
            for g in range(n_grp):
                k = keys_ref[kb, g * SUBLANES:(g + 1) * SUBLANES, :]
                acc = acc + jnp.where(cmp(k, thr8), 1, 0)
            return acc

        acc = lax.fori_loop(0, nkb, blk, jnp.zeros((SUBLANES, TQ), I32))
        return _col_sum(acc)

    ge = lambda a, b: a >= b
    gt = lambda a, b: a > b

    zero = jnp.zeros((1, TQ), I32)
    thr = jnp.where(count(ge, zero) >= TOPK_MAX, zero, jnp.int32(INT_MIN))

    def bit_body(i, thr):
        cand = thr | lax.shift_left(jnp.int32(1), 30 - i)
        return jnp.where(count(ge, cand) >= TOPK_MAX, cand, thr)

    thr = lax.fori_loop(0, 31, bit_body, thr)
    n_gt = count(gt, thr)
    n_ge = count(ge, thr)
    need = (TOPK_MAX - n_gt).astype(F32)
    tie = (n_ge > TOPK_MAX) & (thr > INT_MIN)
    any_tie = jnp.max(jnp.where(tie, 1.0, 0.0)) > 0.0

    @pl.when(jnp.logical_not(any_tie))
    def _():
        floor = jnp.maximum(thr, jnp.int32(INT_MIN + 1))

        def blk(kb, carry):
            mask_ref[kb] = jnp.where(keys_ref[kb] >= floor, 0.0, MASKED)
            return carry

        lax.fori_loop(0, nkb, blk, 0)

    @pl.when(any_tie)
    def _():
        lower = (lax.broadcasted_iota(I32, (KB, KB), 1)
                 < lax.broadcasted_iota(I32, (KB, KB), 0))
        ltri = jnp.where(lower, 1.0, 0.0).astype(BF16)

        def blk(kb, seen):
            k = keys_ref[kb]
            eq = jnp.where(k == thr, 1.0, 0.0)
            rank = _dot(ltri, eq.astype(BF16)) + seen
            keep = (k > thr) | ((k == thr) & (rank < need))
            keep = keep & (k > INT_MIN)
            mask_ref[kb] = jnp.where(keep, 0.0, MASKED)
            return seen + _col_sum(eq)

        lax.fori_loop(0, nkb, blk, jnp.zeros((1, TQ), F32))

    acc_ref[...] = jnp.zeros_like(acc_ref)

    def attn_blk(kb, carry):
        ms, ls = carry
        mask = mask_ref[kb]
        heads = range(N_HEADS_A)
        lgs = [_dot(ka_ref[0, kb, :, (h // 2) * PAIR:(h // 2 + 1) * PAIR],
                    qa_ref[0, h * PAIR:(h + 1) * PAIR, :]) + mask
               for h in heads]
        ms_new = [jnp.maximum(ms[h], _col_max(lgs[h])) for h in heads]
        alphas = [jnp.exp(ms[h] - ms_new[h]) for h in heads]
        es = [jnp.exp(lgs[h] - ms_new[h]) for h in heads]
        ls_new = [ls[h] * alphas[h] + _col_sum(es[h]) for h in heads]
        pvs = [_dot(va_ref[0, kb, h * HEAD_DIM:(h + 1) * HEAD_DIM, :],
                    es[h].astype(BF16)) for h in heads]
        for h in heads:
            acc_ref[h] = acc_ref[h] * alphas[h] + pvs[h]
        return tuple(ms_new), tuple(ls_new)

    m0 = tuple(jnp.full((1, TQ), MASKED, F32) for _ in range(N_HEADS_A))
    l0 = tuple(jnp.zeros((1, TQ), F32) for _ in range(N_HEADS_A))
    _, ls = lax.fori_loop(0, nkb, attn_blk, (m0, l0))
    for p in range(N_HEADS_A // 2):
        outs = [acc_ref[h] / ls[h] for h in (2 * p, 2 * p + 1)]
        out_ref[0, :, p * PAIR:(p + 1) * PAIR] = _finish_pair(outs)


def _dsa_call(qa, ka, va, qi, ki, wi):
    B, nkb = ka.shape[0], ka.shape[1]
    S = nkb * KB

    def tile_t(rows):
        return pl.BlockSpec((1, rows, TQ), lambda b, t: (b, 0, t))

    def whole(a):
        return pl.BlockSpec((1,) + a.shape[1:], lambda b, t: (b, 0, 0, 0))

    return pl.pallas_call(
        _dsa_kernel,
        grid=(B, S // TQ),
        in_specs=[tile_t(N_HEADS_A * PAIR), whole(ka), whole(va),
                  tile_t(IDX_HEADS * PAIR), whole(ki), tile_t(WI_ROWS)],
        out_specs=pl.BlockSpec((1, TQ, WIDTH_A), lambda b, t: (b, t, 0)),
        out_shape=jax.ShapeDtypeStruct((B, S, WIDTH_A), BF16),
        scratch_shapes=[
            pltpu.VMEM((nkb, KB, TQ), I32),
            pltpu.VMEM((nkb, KB, TQ), F32),
            pltpu.VMEM((N_HEADS_A, HEAD_DIM, TQ), F32),
        ],
        compiler_params=pltpu.CompilerParams(
            dimension_semantics=("parallel", "parallel"),
            vmem_limit_bytes=VMEM_LIMIT),
        name="dsa_attn",
    )(qa, ka, va, qi, ki, wi)


def _band_mem_kernel(qb_ref, kb_ref, vb_ref, bias_ref, qm_ref, mk_ref, mv_ref,
                     out_ref):
    t = pl.program_id(1)
    srcs, pens = [], []
    for r in range(BAND_BLOCKS):
        blk = t - (BAND_BLOCKS - 1) + r
        srcs.append(jnp.maximum(blk, 0))
        pens.append(jnp.where(blk >= 0, 0.0, NEG_INF))

    def head_rows(h):
        return slice(h * HEAD_DIM, (h + 1) * HEAD_DIM)

    def pair_cols(h):
        return slice((h // 2) * PAIR, (h // 2 + 1) * PAIR)

    band_lg = [[_dot(kb_ref[0, srcs[r], :, pair_cols(h)],
                     qb_ref[0, h * PAIR:(h + 1) * PAIR, :])
                + bias_ref[0, h, r * KB:(r + 1) * KB, :] + pens[r]
                for r in range(BAND_BLOCKS)] for h in range(N_HEADS_B)]
    mem_lg = [_dot(mk_ref[0, :, pair_cols(h)],
                   qm_ref[0, h * PAIR:(h + 1) * PAIR, :])
              for h in range(N_HEADS_M)]

    band_e, band_l = [], []
    for lgs in band_lg:
        m = functools.reduce(jnp.maximum, [_col_max(lg) for lg in lgs])
        es = [jnp.exp(lg - m) for lg in lgs]
        band_e.append([e.astype(BF16) for e in es])
        band_l.append(functools.reduce(jnp.add, [_col_sum(e) for e in es]))
    mem_e, mem_l = [], []
    for lg in mem_lg:
        e = jnp.exp(lg - _col_max(lg))
        mem_e.append(e.astype(BF16))
        mem_l.append(_col_sum(e))

    band_o = [functools.reduce(jnp.add, [
        _dot(vb_ref[0, srcs[r], head_rows(h), :], band_e[h][r])
        for r in range(BAND_BLOCKS)]) / band_l[h] for h in range(N_HEADS_B)]
    mem_o = [_dot(mv_ref[0, head_rows(h), :], mem_e[h]) / mem_l[h]
             for h in range(N_HEADS_M)]

    for p in range(N_HEADS_B // 2):
        out_ref[0, :, p * PAIR:(p + 1) * PAIR] = _finish_pair(
            band_o[2 * p:2 * p + 2])
    for p in range(N_HEADS_M // 2):
        out_ref[0, :, WIDTH_B + p * PAIR:WIDTH_B + (p + 1) * PAIR] = (
            _finish_pair(mem_o[2 * p:2 * p + 2]))


def _band_mem_call(qb, kb, vb, bias, layer, qm, mk, mv):
    B, nkb = kb.shape[0], kb.shape[1]
    S = nkb * KB
    n_mem = mk.shape[1]

    def tile_t(rows):
        return pl.BlockSpec((1, rows, TQ), lambda b, t: (b, 0, t))

    def whole(a):
        return pl.BlockSpec((1,) + a.shape[1:], lambda b, t: (b, 0, 0, 0))

    return pl.pallas_call(
        _band_mem_kernel,
        grid=(B, S // TQ),
        in_specs=[
            tile_t(N_HEADS_B * PAIR), whole(kb), whole(vb),
            pl.BlockSpec((1,) + bias.shape[1:], lambda b, t: (layer, 0, 0, 0)),
            tile_t(N_HEADS_M * PAIR),
            pl.BlockSpec((1, n_mem, WIDTH_M), lambda b, t: (b, 0, 0)),
            pl.BlockSpec((1, WIDTH_M, n_mem), lambda b, t: (b, 0, 0)),
        ],
        out_specs=pl.BlockSpec((1, TQ, WIDTH_B + WIDTH_M), lambda b, t: (b, t, 0)),
        out_shape=jax.ShapeDtypeStruct((B, S, WIDTH_B + WIDTH_M), BF16),
        compiler_params=pltpu.CompilerParams(
            dimension_semantics=("parallel", "parallel"),
            vmem_limit_bytes=VMEM_LIMIT),
        name="band_mem_attn",
    )(qb, kb, vb, bias, qm, mk, mv)


FF_CHUNK = D_FF // 2


def _out_ffn_kernel(x_ref, oa_ref, obm_ref, wo_ref, gffn_ref, wgu_ref, wd_ref,
                    out_ref):
    attn = jnp.concatenate([oa_ref[...], obm_ref[...]], axis=1)
    x1 = x_ref[...] + _dot(attn, wo_ref[...])
    h = _row_rms(x1, gffn_ref[...]).astype(BF16)
    acc = x1
    for c in range(D_FF // FF_CHUNK):
        c0 = c * FF_CHUNK
        gate = _dot(h, wgu_ref[:, c0:c0 + FF_CHUNK])
        up = _dot(h, wgu_ref[:, D_FF + c0:D_FF + c0 + FF_CHUNK])
        act = gate * (1.0 / (1.0 + jnp.exp(-gate))) * up
        acc = acc + _dot(act.astype(BF16), wd_ref[c0:c0 + FF_CHUNK, :])
    out_ref[...] = acc


def _out_ffn_call(x2, oa2, obm2, wo, gffn, wgu, wd):
    n = x2.shape[0]

    def rows(width):
        return pl.BlockSpec((TM, width), lambda i: (i, 0))

    def resident(a):
        return pl.BlockSpec(a.shape, lambda i: (0, 0),
                            pipeline_mode=pl.Buffered(1))

    return pl.pallas_call(
        _out_ffn_kernel,
        grid=(n // TM,),
        in_specs=[rows(D_MODEL), rows(WIDTH_A), rows(WIDTH_B + WIDTH_M),
                  resident(wo), resident(gffn), resident(wgu), resident(wd)],
        out_specs=rows(D_MODEL),
        out_shape=jax.ShapeDtypeStruct((n, D_MODEL), F32),
        compiler_params=pltpu.CompilerParams(
            dimension_semantics=("parallel",), vmem_limit_bytes=VMEM_LIMIT),
        name="out_ffn",
    )(x2, oa2, obm2, wo, gffn, wgu, wd)


def kernel(x, mem, positions, g_mix, w_in, g_q_a, g_k_a, g_k_idx, g_q_b, g_k_b,
           rel_bias, g_q_m, g_k_m, g_mem, w_mem_kv, w_out, g_ffn, w_gate_up,
           w_down):
    B, S, D = x.shape
    depth = w_in.shape[0]

    w_in_t = jnp.swapaxes(w_in, 1, 2)
    n_wi = R_KI + IDX_DIM + IDX_HEADS
    w_in_t = jnp.concatenate(
        [w_in_t[:, :n_wi],
         jnp.zeros((depth, WI_PAD - IDX_HEADS, D), w_in.dtype),
         w_in_t[:, n_wi:]], axis=1).astype(BF16)
    w_mem_t = jnp.swapaxes(w_mem_kv, 1, 2).astype(BF16)
    w_out_b = w_out.astype(BF16)
    w_gu_b = w_gate_up.astype(BF16)
    w_down_b = w_down.astype(BF16)
    gains = jnp.stack(
        [g_q_a, g_k_a, g_k_idx, g_q_b, g_k_b, g_q_m], axis=1)[..., None]

    inv_freq = jnp.power(
        ROPE_THETA, -jnp.arange(ROT_HALF, dtype=F32) / ROT_HALF)[:, None]
    pos3 = positions.reshape(B, 1, S)

    n_f = 2 * REL_CLIP
    fvec = jnp.concatenate(
        [rel_bias[..., 1:], jnp.broadcast_to(rel_bias[..., -1:],
                                             rel_bias.shape[:-1] + (n_f,))],
        axis=-1)[:, :, None, :]
    bias = _bias_call(fvec)

    for l in range(depth):
        qa, ka, va, qi, ki, wi, qb, kb, vb, qm = _proj_call(
            x, pos3, inv_freq, g_mix[l][None], w_in_t[l], gains[l])
        mk, mv = _memkv_call(mem, g_mem[l][None], w_mem_t[l], g_k_m[l][:, None])
        oa = _dsa_call(qa, ka, va, qi, ki, wi)
        obm = _band_mem_call(qb, kb, vb, bias, l, qm, mk, mv)
        x = _out_ffn_call(
            x.reshape(B * S, D), oa.reshape(B * S, WIDTH_A),
            obm.reshape(B * S, WIDTH_B + WIDTH_M), w_out_b[l], g_ffn[l][None],
            w_gu_b[l], w_down_b[l]).reshape(B, S, D)
    return x
```

```python
import functools

import jax
import jax.numpy as jnp
from jax import lax
from jax.experimental import pallas as pl
from jax.experimental.pallas import tpu as pltpu

F32 = jnp.float32
BF16 = jnp.bfloat16
I32 = jnp.int32
I16 = jnp.int16

D_MODEL = 1024
HEAD_DIM = 64
N_HEADS_A = 6
N_HEADS_B = 6
N_HEADS_M = 4
WIDTH_A = N_HEADS_A * HEAD_DIM
WIDTH_B = N_HEADS_B * HEAD_DIM
WIDTH_M = N_HEADS_M * HEAD_DIM
IDX_HEADS = 4
IDX_DIM = 64
ROT_DIM = HEAD_DIM // 4
ROT_HALF = ROT_DIM // 2
ROPE_THETA = 500000.0
CHUNK = 64
PREV_CHUNKS = 8
REL_CLIP = 256
TOPK_MAX = 256
D_FF = 2816
EPS = 1e-6

LANES = 128
SUBLANES = 8
PAIR = 2 * HEAD_DIM
TM = 512
TQ = 256
KB = 256
BAND_BLOCKS = 3
WI_ROWS = 8
WI_PAD = 16
INT_MIN = -(2 ** 31)
HALF16 = 2 ** 15
PACK16 = 16
NEG_INF = float("-inf")
MASKED = -1e30
VMEM_LIMIT = 56 * 1024 * 1024

R_QA = 0
R_KA = R_QA + WIDTH_A
R_VA = R_KA + WIDTH_A
R_QI = R_VA + WIDTH_A
R_KI = R_QI + IDX_HEADS * IDX_DIM
R_WI = R_KI + IDX_DIM
R_QB = R_WI + WI_PAD
R_KB = R_QB + WIDTH_B
R_VB = R_KB + WIDTH_B
R_QM = R_VB + WIDTH_B
R_END = R_QM + WIDTH_M

NT_DIMS = (((1,), (1,)), ((), ()))


def _dot(a, b):
    return jnp.dot(a, b, preferred_element_type=F32)


def _dot_nt(a, b):
    return lax.dot_general(a, b, NT_DIMS, preferred_element_type=F32)


def _chunk_of(pos):
    return jnp.right_shift(pos, CHUNK.bit_length() - 1)


def _row_rms(x, g):
    ms = jnp.mean(x * x, axis=-1, keepdims=True)
    return x * lax.rsqrt(ms + EPS) * g


def _head_norm_t(blk, g_col):
    ms = jnp.mean(blk * blk, axis=0, keepdims=True)
    return blk * lax.rsqrt(ms + EPS) * g_col


def _rope_t(y, cos, sin):
    y1 = y[0:ROT_HALF]
    y2 = y[ROT_HALF:ROT_DIM]
    return jnp.concatenate(
        [y1 * cos - y2 * sin, y2 * cos + y1 * sin, y[ROT_DIM:]], axis=0)


def _pair_slot(y, odd):
    z = jnp.zeros_like(y)
    return jnp.concatenate([z, y] if odd else [y, z], axis=0)


def _proj_kernel(x_ref, pos_ref, invf_ref, gmix_ref, wt_ref, gains_ref,
                 qa_ref, ka_ref, va_ref, qi_ref, ki_ref, wi_ref,
                 qb_ref, kb_ref, vb_ref, qm_ref):
    h = _row_rms(x_ref[0], gmix_ref[...]).astype(BF16)
    ang = invf_ref[...] * pos_ref[0].astype(F32)
    cos = jnp.cos(ang)
    sin = jnp.sin(ang)
    q_scale = HEAD_DIM ** -0.5
    n_blk = TM // KB

    def proj_t(r0, r1):
        return _dot_nt(wt_ref[r0:r1, :], h)

    def gain(i):
        return gains_ref[i]

    def head(t, i):
        return t[i * HEAD_DIM:(i + 1) * HEAD_DIM]

    def store_q(ref, t, n_heads, g, rope):
        for i in range(n_heads):
            y = head(t, i)
            if g is not None:
                y = _head_norm_t(y, g)
            if rope:
                y = _rope_t(y, cos, sin)
            ref[0, i * PAIR:(i + 1) * PAIR, :] = _pair_slot(
                y * q_scale, i % 2 == 1).astype(BF16)

    def store_k(ref, t, n_heads, g, rope):
        for p in range(n_heads // 2):
            ys = []
            for i in (2 * p, 2 * p + 1):
                y = _head_norm_t(head(t, i), g)
                ys.append(_rope_t(y, cos, sin) if rope else y)
            blk = jnp.concatenate(ys, axis=0).T.astype(BF16)
            for j in range(n_blk):
                ref[0, j, :, p * PAIR:(p + 1) * PAIR] = blk[j * KB:(j + 1) * KB]

    def store_v(ref, t):
        tb = t.astype(BF16)
        for j in range(n_blk):
            ref[0, j] = tb[:, j * KB:(j + 1) * KB]

    store_q(qa_ref, proj_t(R_QA, R_KA), N_HEADS_A, gain(0), True)
    store_k(ka_ref, proj_t(R_KA, R_VA), N_HEADS_A, gain(1), True)
    store_v(va_ref, proj_t(R_VA, R_QI))

    t = proj_t(R_QI, R_QB)
    for i in range(IDX_HEADS):
        y = _rope_t(head(t, i), cos, sin) * (IDX_DIM ** -0.5)
        qi_ref[0, i * PAIR:(i + 1) * PAIR, :] = _pair_slot(y, False).astype(BF16)
    ki = _rope_t(_head_norm_t(head(t, IDX_HEADS), gain(2)), cos, sin)
    ki_blk = _pair_slot(ki, False).T.astype(BF16)
    for j in range(n_blk):
        ki_ref[0, j] = ki_blk[j * KB:(j + 1) * KB]
    wi_ref[0] = t[R_WI - R_QI:R_WI - R_QI + WI_ROWS] * (IDX_HEADS ** -0.5)

    store_q(qb_ref, proj_t(R_QB, R_KB), N_HEADS_B, gain(3), False)
    store_k(kb_ref, proj_t(R_KB, R_VB), N_HEADS_B, gain(4), False)
    store_v(vb_ref, proj_t(R_VB, R_QM))
    store_q(qm_ref, proj_t(R_QM, R_END), N_HEADS_M, gain(5), False)


def _proj_call(x, pos3, invf, gmix, wt, gains):
    B, S, _ = x.shape
    nkb = S // KB
    n_blk = TM // KB

    def tok_t(rows):
        return pl.BlockSpec((1, rows, TM), lambda b, i: (b, 0, i))

    def full(shape):
        return pl.BlockSpec(shape, lambda b, i: (0,) * len(shape))

    def krow(width):
        return pl.BlockSpec((1, n_blk, KB, width), lambda b, i: (b, i, 0, 0))

    def vt(rows):
        return pl.BlockSpec((1, n_blk, rows, KB), lambda b, i: (b, i, 0, 0))

    out_shape = (
        jax.ShapeDtypeStruct((B, N_HEADS_A * PAIR, S), BF16),
        jax.ShapeDtypeStruct((B, nkb, KB, WIDTH_A), BF16),
        jax.ShapeDtypeStruct((B, nkb, WIDTH_A, KB), BF16),
        jax.ShapeDtypeStruct((B, IDX_HEADS * PAIR, S), BF16),
        jax.ShapeDtypeStruct((B, nkb, KB, PAIR), BF16),
        jax.ShapeDtypeStruct((B, WI_ROWS, S), F32),
        jax.ShapeDtypeStruct((B, N_HEADS_B * PAIR, S), BF16),
        jax.ShapeDtypeStruct((B, nkb, KB, WIDTH_B), BF16),
        jax.ShapeDtypeStruct((B, nkb, WIDTH_B, KB), BF16),
        jax.ShapeDtypeStruct((B, N_HEADS_M * PAIR, S), BF16),
    )
    out_specs = (
        tok_t(N_HEADS_A * PAIR), krow(WIDTH_A), vt(WIDTH_A),
        tok_t(IDX_HEADS * PAIR), krow(PAIR), tok_t(WI_ROWS),
        tok_t(N_HEADS_B * PAIR), krow(WIDTH_B), vt(WIDTH_B),
        tok_t(N_HEADS_M * PAIR),
    )
    return pl.pallas_call(
        _proj_kernel,
        grid=(B, S // TM),
        in_specs=[
            pl.BlockSpec((1, TM, D_MODEL), lambda b, i: (b, i, 0)),
            pl.BlockSpec((1, 1, TM), lambda b, i: (b, 0, i)),
            full((ROT_HALF, 1)),
            full((1, D_MODEL)),
            full((R_END, D_MODEL)),
            full((6, HEAD_DIM, 1)),
        ],
        out_specs=out_specs,
        out_shape=out_shape,
        compiler_params=pltpu.CompilerParams(
            dimension_semantics=("parallel", "parallel"),
            vmem_limit_bytes=VMEM_LIMIT),
        name="in_proj",
    )(x, pos3, invf, gmix, wt, gains)


def _memkv_kernel(mem_ref, gmem_ref, wt_ref, gk_ref, mk_ref, mv_ref):
    hm = _row_rms(mem_ref[0], gmem_ref[...]).astype(BF16)
    kt = _dot_nt(wt_ref[0:WIDTH_M, :], hm)
    for p in range(N_HEADS_M // 2):
        ys = [_head_norm_t(kt[i * HEAD_DIM:(i + 1) * HEAD_DIM], gk_ref[...])
              for i in (2 * p, 2 * p + 1)]
        mk_ref[0, :, p * PAIR:(p + 1) * PAIR] = (
            jnp.concatenate(ys, axis=0).T.astype(BF16))
    mv_ref[0] = _dot_nt(wt_ref[WIDTH_M:2 * WIDTH_M, :], hm).astype(BF16)


def _memkv_call(mem, gmem, wt, gk):
    B, n_mem, _ = mem.shape
    return pl.pallas_call(
        _memkv_kernel,
        grid=(B,),
        in_specs=[
            pl.BlockSpec((1, n_mem, D_MODEL), lambda b: (b, 0, 0)),
            pl.BlockSpec((1, D_MODEL), lambda b: (0, 0)),
            pl.BlockSpec((2 * WIDTH_M, D_MODEL), lambda b: (0, 0)),
            pl.BlockSpec((HEAD_DIM, 1), lambda b: (0, 0)),
        ],
        out_specs=(
            pl.BlockSpec((1, n_mem, WIDTH_M), lambda b: (b, 0, 0)),
            pl.BlockSpec((1, WIDTH_M, n_mem), lambda b: (b, 0, 0)),
        ),
        out_shape=(
            jax.ShapeDtypeStruct((B, n_mem, WIDTH_M), BF16),
            jax.ShapeDtypeStruct((B, WIDTH_M, n_mem), BF16),
        ),
        compiler_params=pltpu.CompilerParams(
            dimension_semantics=("parallel",), vmem_limit_bytes=VMEM_LIMIT),
        name="mem_kv",
    )(mem, gmem, wt, gk)


def _bias_kernel(f_ref, out_ref):
    n_keys = BAND_BLOCKS * KB
    width = f_ref.shape[-1]
    base = jnp.broadcast_to(f_ref[0, 0], (n_keys, width))
    rolled = pltpu.roll(base, width - (n_keys - 1), 1, stride=1, stride_axis=0)
    tab = rolled[:, 0:TQ]
    key_c = _chunk_of(lax.broadcasted_iota(I32, (n_keys, TQ), 0))
    qry_c = _chunk_of(lax.broadcasted_iota(I32, (n_keys, TQ), 1)
                      + (BAND_BLOCKS - 1) * KB)
    ok = (key_c <= qry_c) & (key_c >= qry_c - PREV_CHUNKS)
    out_ref[0, 0] = jnp.where(ok, tab, NEG_INF)


def _bias_call(fvec):
    depth, n_heads, _, width = fvec.shape
    n_keys = BAND_BLOCKS * KB
    return pl.pallas_call(
        _bias_kernel,
        grid=(depth, n_heads),
        in_specs=[pl.BlockSpec((1, 1, 1, width), lambda l, h: (l, h, 0, 0))],
        out_specs=pl.BlockSpec((1, 1, n_keys, TQ), lambda l, h: (l, h, 0, 0)),
        out_shape=jax.ShapeDtypeStruct((depth, n_heads, n_keys, TQ), F32),
        compiler_params=pltpu.CompilerParams(
            dimension_semantics=("parallel", "parallel")),
        name="band_bias",
    )(fvec)


def _col_max(x):
    return jnp.max(x, axis=0, keepdims=True)


def _col_sum(x):
    return jnp.sum(x, axis=0, keepdims=True)


def _finish_pair(outs):
    return jnp.concatenate(outs, axis=0).T.astype(BF16)


def _dsa_kernel(qa_ref, ka_ref, va_ref, qi_ref, ki_ref, wi_ref, out_ref,
                keys_ref, half_ref, mask_ref, acc_ref):
    t = pl.program_id(1)
    nkb = t + 1
    n_grp = KB // SUBLANES
    qry_c = _chunk_of(t * TQ + lax.broadcasted_iota(I32, (KB, TQ), 1))
    key_c0 = _chunk_of(lax.broadcasted_iota(I32, (KB, TQ), 0))

    def score_blk(kb, carry):
        kblk = ki_ref[0, kb]
        s = jnp.zeros((KB, TQ), F32)
        for i in range(IDX_HEADS):
            d = _dot(kblk, qi_ref[0, i * PAIR:(i + 1) * PAIR, :])
            s = s + jnp.maximum(d, 0.0) * wi_ref[0, i:i + 1, :]
        s = jnp.where(s == 0.0, 0.0, s)
        bits = lax.bitcast_convert_type(s, I32)
        key = jnp.where(bits >= 0, bits, bits ^ jnp.int32(0x7FFFFFFF))
        allowed = (key_c0 + kb * (KB // CHUNK)) <= qry_c
        key = jnp.where(allowed, key, jnp.int32(INT_MIN))
        keys_ref[kb] = key
        half_ref[kb] = jnp.right_shift(key, 16).astype(I16)
        return carry

    lax.fori_loop(0, nkb, score_blk, 0)

    def count(cmp, thr):
        thr8 = jnp.broadcast_to(thr, (SUBLANES, TQ))

        def blk(kb, acc):
            for g in range(n_grp):
                k = keys_ref[kb, g * SUBLANES:(g + 1) * SUBLANES, :]
                acc = acc + jnp.where(cmp(k, thr8), 1, 0)
            return acc

        acc = lax.fori_loop(0, nkb, blk, jnp.zeros((SUBLANES, TQ), I32))
        return _col_sum(acc)

    def count16(thr):
        thr16 = jnp.broadcast_to(thr.astype(I16), (PACK16, TQ))

        def blk(kb, acc):
            for g in range(KB // PACK16):
                k = half_ref[kb, g * PACK16:(g + 1) * PACK16, :]
                acc = acc + jnp.where(k >= thr16, jnp.int16(1), jnp.int16(0))
            return acc

        acc = lax.fori_loop(0, nkb, blk, jnp.zeros((PACK16, TQ), I16))
        return _col_sum(acc.astype(I32))

    gt = lambda a, b: a > b

    def search(first_bit, n_bits, offset, carry):
        def bit(i, carry):
            thr, n = carry
            cand = thr | lax.shift_left(jnp.int32(1), first_bit - i)
            c = count16(cand - offset)
            ok = c >= TOPK_MAX
            return jnp.where(ok, cand, thr), jnp.where(ok, c, n)

        return lax.fori_loop(0, n_bits, bit, carry)

    zero = jnp.zeros((1, TQ), I32)
    c0 = count16(zero)
    thi = jnp.where(c0 >= TOPK_MAX, zero, jnp.int32(-HALF16))
    n_ge = jnp.where(c0 >= TOPK_MAX, c0, nkb * KB)
    thi, n_ge = search(14, 15, 0, (thi, n_ge))

    def lo_blk(kb, carry):
        k = keys_ref[kb]
        hi = jnp.right_shift(k, 16)
        lo = (k & jnp.int32(0xFFFF)) - HALF16
        lo = jnp.where(hi == thi, lo,
                       jnp.where(hi > thi, jnp.int32(HALF16 - 1),
                                 jnp.int32(-HALF16)))
        half_ref[kb] = lo.astype(I16)
        return carry

    lax.fori_loop(0, nkb, lo_blk, 0)
    tlo, n_ge = search(15, 16, HALF16, (zero, n_ge))
    thr = lax.shift_left(thi, 16) | tlo
    tie = (n_ge > TOPK_MAX) & (thr > INT_MIN)
    any_tie = jnp.max(jnp.where(tie, 1.0, 0.0)) > 0.0

    @pl.when(jnp.logical_not(any_tie))
    def _():
        floor = jnp.maximum(thr, jnp.int32(INT_MIN + 1))

        def blk(kb, carry):
            mask_ref[kb] = jnp.where(keys_ref[kb] >= floor, 0.0, MASKED)
            return carry

        lax.fori_loop(0, nkb, blk, 0)

    @pl.when(any_tie)
    def _():
        lower = (lax.broadcasted_iota(I32, (KB, KB), 1)
                 < lax.broadcasted_iota(I32, (KB, KB), 0))
        ltri = jnp.where(lower, 1.0, 0.0).astype(BF16)
        need = (TOPK_MAX - count(gt, thr)).astype(F32)

        def blk(kb, seen):
            k = keys_ref[kb]
            eq = jnp.where(k == thr, 1.0, 0.0)
            rank = _dot(ltri, eq.astype(BF16)) + seen
            keep = (k > thr) | ((k == thr) & (rank < need))
            keep = keep & (k > INT_MIN)
            mask_ref[kb] = jnp.where(keep, 0.0, MASKED)
            return seen + _col_sum(eq)

        lax.fori_loop(0, nkb, blk, jnp.zeros((1, TQ), F32))

    acc_ref[...] = jnp.zeros_like(acc_ref)

    def attn_blk(kb, carry):
        ms, ls = carry
        mask = mask_ref[kb]
        heads = range(N_HEADS_A)
        lgs = [_dot(ka_ref[0, kb, :, (h // 2) * PAIR:(h // 2 + 1) * PAIR],
                    qa_ref[0, h * PAIR:(h + 1) * PAIR, :]) + mask
               for h in heads]
        ms_new = [jnp.maximum(ms[h], _col_max(lgs[h])) for h in heads]
        alphas = [jnp.exp(ms[h] - ms_new[h]) for h in heads]
        es = [jnp.exp(lgs[h] - ms_new[h]) for h in heads]
        ls_new = [ls[h] * alphas[h] + _col_sum(es[h]) for h in heads]
        pvs = [_dot(va_ref[0, kb, h * HEAD_DIM:(h + 1) * HEAD_DIM, :],
                    es[h].astype(BF16)) for h in heads]
        for h in heads:
            acc_ref[h] = acc_ref[h] * alphas[h] + pvs[h]
        return tuple(ms_new), tuple(ls_new)

    m0 = tuple(jnp.full((1, TQ), MASKED, F32) for _ in range(N_HEADS_A))
    l0 = tuple(jnp.zeros((1, TQ), F32) for _ in range(N_HEADS_A))
    _, ls = lax.fori_loop(0, nkb, attn_blk, (m0, l0))
    for p in range(N_HEADS_A // 2):
        outs = [acc_ref[h] / ls[h] for h in (2 * p, 2 * p + 1)]
        out_ref[0, :, p * PAIR:(p + 1) * PAIR] = _finish_pair(outs)


def _dsa_call(qa, ka, va, qi, ki, wi):
    B, nkb = ka.shape[0], ka.shape[1]
    S = nkb * KB

    def tile_t(rows):
        return pl.BlockSpec((1, rows, TQ), lambda b, t: (b, 0, t))

    def whole(a):
        return pl.BlockSpec((1,) + a.shape[1:], lambda b, t: (b, 0, 0, 0))

    return pl.pallas_call(
        _dsa_kernel,
        grid=(B, S // TQ),
        in_specs=[tile_t(N_HEADS_A * PAIR), whole(ka), whole(va),
                  tile_t(IDX_HEADS * PAIR), whole(ki), tile_t(WI_ROWS)],
        out_specs=pl.BlockSpec((1, TQ, WIDTH_A), lambda b, t: (b, t, 0)),
        out_shape=jax.ShapeDtypeStruct((B, S, WIDTH_A), BF16),
        scratch_shapes=[
            pltpu.VMEM((nkb, KB, TQ), I32),
            pltpu.VMEM((nkb, KB, TQ), I16),
            pltpu.VMEM((nkb, KB, TQ), F32),
            pltpu.VMEM((N_HEADS_A, HEAD_DIM, TQ), F32),
        ],
        compiler_params=pltpu.CompilerParams(
            dimension_semantics=("parallel", "parallel"),
            vmem_limit_bytes=VMEM_LIMIT),
        name="dsa_attn",
    )(qa, ka, va, qi, ki, wi)


def _band_mem_kernel(qb_ref, kb_ref, vb_ref, bias_ref, qm_ref, mk_ref, mv_ref,
                     out_ref):
    t = pl.program_id(1)
    srcs, pens = [], []
    for r in range(BAND_BLOCKS):
        blk = t - (BAND_BLOCKS - 1) + r
        srcs.append(jnp.maximum(blk, 0))
        pens.append(jnp.where(blk >= 0, 0.0, NEG_INF))

    def head_rows(h):
        return slice(h * HEAD_DIM, (h + 1) * HEAD_DIM)

    def pair_cols(h):
        return slice((h // 2) * PAIR, (h // 2 + 1) * PAIR)

    band_lg = [[_dot(kb_ref[0, srcs[r], :, pair_cols(h)],
                     qb_ref[0, h * PAIR:(h + 1) * PAIR, :])
                + bias_ref[0, h, r * KB:(r + 1) * KB, :] + pens[r]
                for r in range(BAND_BLOCKS)] for h in range(N_HEADS_B)]
    mem_lg = [_dot(mk_ref[0, :, pair_cols(h)],
                   qm_ref[0, h * PAIR:(h + 1) * PAIR, :])
              for h in range(N_HEADS_M)]

    band_e, band_l = [], []
    for lgs in band_lg:
        m = functools.reduce(jnp.maximum, [_col_max(lg) for lg in lgs])
        es = [jnp.exp(lg - m) for lg in lgs]
        band_e.append([e.astype(BF16) for e in es])
        band_l.append(functools.reduce(jnp.add, [_col_sum(e) for e in es]))
    mem_e, mem_l = [], []
    for lg in mem_lg:
        e = jnp.exp(lg - _col_max(lg))
        mem_e.append(e.astype(BF16))
        mem_l.append(_col_sum(e))

    band_o = [functools.reduce(jnp.add, [
        _dot(vb_ref[0, srcs[r], head_rows(h), :], band_e[h][r])
        for r in range(BAND_BLOCKS)]) / band_l[h] for h in range(N_HEADS_B)]
    mem_o = [_dot(mv_ref[0, head_rows(h), :], mem_e[h]) / mem_l[h]
             for h in range(N_HEADS_M)]

    for p in range(N_HEADS_B // 2):
        out_ref[0, :, p * PAIR:(p + 1) * PAIR] = _finish_pair(
            band_o[2 * p:2 * p + 2])
    for p in range(N_HEADS_M // 2):
        out_ref[0, :, WIDTH_B + p * PAIR:WIDTH_B + (p + 1) * PAIR] = (
            _finish_pair(mem_o[2 * p:2 * p + 2]))


def _band_mem_call(qb, kb, vb, bias, layer, qm, mk, mv):
    B, nkb = kb.shape[0], kb.shape[1]
    S = nkb * KB
    n_mem = mk.shape[1]

    def tile_t(rows):
        return pl.BlockSpec((1, rows, TQ), lambda b, t: (b, 0, t))

    def whole(a):
        return pl.BlockSpec((1,) + a.shape[1:], lambda b, t: (b, 0, 0, 0))

    return pl.pallas_call(
        _band_mem_kernel,
        grid=(B, S // TQ),
        in_specs=[
            tile_t(N_HEADS_B * PAIR), whole(kb), whole(vb),
            pl.BlockSpec((1,) + bias.shape[1:], lambda b, t: (layer, 0, 0, 0)),
            tile_t(N_HEADS_M * PAIR),
            pl.BlockSpec((1, n_mem, WIDTH_M), lambda b, t: (b, 0, 0)),
            pl.BlockSpec((1, WIDTH_M, n_mem), lambda b, t: (b, 0, 0)),
        ],
        out_specs=pl.BlockSpec((1, TQ, WIDTH_B + WIDTH_M), lambda b, t: (b, t, 0)),
        out_shape=jax.ShapeDtypeStruct((B, S, WIDTH_B + WIDTH_M), BF16),
        compiler_params=pltpu.CompilerParams(
            dimension_semantics=("parallel", "parallel"),
            vmem_limit_bytes=VMEM_LIMIT),
        name="band_mem_attn",
    )(qb, kb, vb, bias, qm, mk, mv)


FF_CHUNK = D_FF // 2


def _out_ffn_kernel(x_ref, oa_ref, obm_ref, wo_ref, gffn_ref, wgu_ref, wd_ref,
                    out_ref):
    attn = jnp.concatenate([oa_ref[...], obm_ref[...]], axis=1)
    x1 = x_ref[...] + _dot(attn, wo_ref[...])
    h = _row_rms(x1, gffn_ref[...]).astype(BF16)
    acc = x1
    for c in range(D_FF // FF_CHUNK):
        c0 = c * FF_CHUNK
        gate = _dot(h, wgu_ref[:, c0:c0 + FF_CHUNK])
        up = _dot(h, wgu_ref[:, D_FF + c0:D_FF + c0 + FF_CHUNK])
        act = gate * (1.0 / (1.0 + jnp.exp(-gate))) * up
        acc = acc + _dot(act.astype(BF16), wd_ref[c0:c0 + FF_CHUNK, :])
    out_ref[...] = acc


def _out_ffn_call(x2, oa2, obm2, wo, gffn, wgu, wd):
    n = x2.shape[0]

    def rows(width):
        return pl.BlockSpec((TM, width), lambda i: (i, 0))

    def resident(a):
        return pl.BlockSpec(a.shape, lambda i: (0, 0),
                            pipeline_mode=pl.Buffered(1))

    return pl.pallas_call(
        _out_ffn_kernel,
        grid=(n // TM,),
        in_specs=[rows(D_MODEL), rows(WIDTH_A), rows(WIDTH_B + WIDTH_M),
                  resident(wo), resident(gffn), resident(wgu), resident(wd)],
        out_specs=rows(D_MODEL),
        out_shape=jax.ShapeDtypeStruct((n, D_MODEL), F32),
        compiler_params=pltpu.CompilerParams(
            dimension_semantics=("parallel",), vmem_limit_bytes=VMEM_LIMIT),
        name="out_ffn",
    )(x2, oa2, obm2, wo, gffn, wgu, wd)


def kernel(x, mem, positions, g_mix, w_in, g_q_a, g_k_a, g_k_idx, g_q_b, g_k_b,
           rel_bias, g_q_m, g_k_m, g_mem, w_mem_kv, w_out, g_ffn, w_gate_up,
           w_down):
    B, S, D = x.shape
    depth = w_in.shape[0]

    w_in_t = jnp.swapaxes(w_in.astype(BF16), 1, 2)
    n_wi = R_KI + IDX_DIM + IDX_HEADS
    w_in_t = jnp.concatenate(
        [w_in_t[:, :n_wi],
         jnp.zeros((depth, WI_PAD - IDX_HEADS, D), BF16),
         w_in_t[:, n_wi:]], axis=1)
    w_mem_t = jnp.swapaxes(w_mem_kv.astype(BF16), 1, 2)
    w_out_b = w_out.astype(BF16)
    w_gu_b = w_gate_up.astype(BF16)
    w_down_b = w_down.astype(BF16)
    gains = jnp.stack(
        [g_q_a, g_k_a, g_k_idx, g_q_b, g_k_b, g_q_m], axis=1)[..., None]

    inv_freq = jnp.power(
        ROPE_THETA, -jnp.arange(ROT_HALF, dtype=F32) / ROT_HALF)[:, None]
    pos3 = positions.reshape(B, 1, S)

    n_f = 2 * REL_CLIP
    fvec = jnp.concatenate(
        [rel_bias[..., 1:], jnp.broadcast_to(rel_bias[..., -1:],
                                             rel_bias.shape[:-1] + (n_f,))],
        axis=-1)[:, :, None, :]
    bias = _bias_call(fvec)

    for l in range(depth):
        qa, ka, va, qi, ki, wi, qb, kb, vb, qm = _proj_call(
            x, pos3, inv_freq, g_mix[l][None], w_in_t[l], gains[l])
        mk, mv = _memkv_call(mem, g_mem[l][None], w_mem_t[l], g_k_m[l][:, None])
        oa = _dsa_call(qa, ka, va, qi, ki, wi)
        obm = _band_mem_call(qb, kb, vb, bias, l, qm, mk, mv)
        x = _out_ffn_call(
            x.reshape(B * S, D), oa.reshape(B * S, WIDTH_A),
            obm.reshape(B * S, WIDTH_B + WIDTH_M), w_out_b[l], g_ffn[l][None],
            w_gu_b[l], w_down_b[l]).reshape(B, S, D)
    return x
```

```python
import functools

import jax
import jax.numpy as jnp
from jax import lax
from jax.experimental import pallas as pl
from jax.experimental.pallas import tpu as pltpu

F32 = jnp.float32
BF16 = jnp.bfloat16
I32 = jnp.int32

D_MODEL = 1024
HEAD_DIM = 64
N_HEADS_A = 6
N_HEADS_B = 6
N_HEADS_M = 4
WIDTH_A = N_HEADS_A * HEAD_DIM
WIDTH_B = N_HEADS_B * HEAD_DIM
WIDTH_M = N_HEADS_M * HEAD_DIM
IDX_HEADS = 4
IDX_DIM = 64
ROT_DIM = HEAD_DIM // 4
ROT_HALF = ROT_DIM // 2
ROPE_THETA = 500000.0
CHUNK = 64
PREV_CHUNKS = 8
REL_CLIP = 256
TOPK_MAX = 256
D_FF = 2816
EPS = 1e-6
LOG2E = 1.4426950408889634

LANES = 128
SUBLANES = 8
PAIR = 2 * HEAD_DIM
TM = 512
TQ = 256
KB = 256
BAND_BLOCKS = 3
WI_ROWS = 8
WI_PAD = 16
INT_MIN = -(2 ** 31)
KEY_BITS = 32
NEG_INF = float("-inf")
MASKED = -1e30
VMEM_LIMIT = 56 * 1024 * 1024

R_QA = 0
R_KA = R_QA + WIDTH_A
R_VA = R_KA + WIDTH_A
R_QI = R_VA + WIDTH_A
R_KI = R_QI + IDX_HEADS * IDX_DIM
R_WI = R_KI + IDX_DIM
R_QB = R_WI + WI_PAD
R_KB = R_QB + WIDTH_B
R_VB = R_KB + WIDTH_B
R_QM = R_VB + WIDTH_B
R_END = R_QM + WIDTH_M

NT_DIMS = (((1,), (1,)), ((), ()))


def _dot(a, b):
    return jnp.dot(a, b, preferred_element_type=F32)


def _dot_nt(a, b):
    return lax.dot_general(a, b, NT_DIMS, preferred_element_type=F32)


def _chunk_of(pos):
    return jnp.right_shift(pos, CHUNK.bit_length() - 1)


def _tree_sum(xs):
    while len(xs) > 1:
        xs = [a + b for a, b in zip(xs[0::2], xs[1::2])] + xs[len(xs) & ~1:]
    return xs[0]


def _as_i32(v):
    return v - (1 << 32) if v >= (1 << 31) else v


def _bit_transpose32(rows):
    rows = list(rows)
    j, m = 16, 0x0000FFFF
    while j:
        k = 0
        while k < 32:
            t = ((rows[k] ^ lax.shift_right_logical(rows[k + j], jnp.int32(j)))
                 & jnp.int32(_as_i32(m)))
            rows[k] = rows[k] ^ t
            rows[k + j] = rows[k + j] ^ lax.shift_left(t, jnp.int32(j))
            k = (k + j + 1) & ~j
        j >>= 1
        if j:
            m = (m ^ (m << j)) & 0xFFFFFFFF
    return rows


def _row_rms(x, g):
    ms = jnp.mean(x * x, axis=-1, keepdims=True)
    return x * lax.rsqrt(ms + EPS) * g


def _head_norm_t(blk, g_col):
    ms = jnp.mean(blk * blk, axis=0, keepdims=True)
    return blk * lax.rsqrt(ms + EPS) * g_col


def _rope_t(y, cos, sin):
    y1 = y[0:ROT_HALF]
    y2 = y[ROT_HALF:ROT_DIM]
    return jnp.concatenate(
        [y1 * cos - y2 * sin, y2 * cos + y1 * sin, y[ROT_DIM:]], axis=0)


def _pair_slot(y, odd):
    z = jnp.zeros_like(y)
    return jnp.concatenate([z, y] if odd else [y, z], axis=0)


def _proj_kernel(x_ref, pos_ref, invf_ref, gmix_ref, wt_ref, gains_ref,
                 qa_ref, ka_ref, va_ref, qi_ref, ki_ref, wi_ref,
                 qb_ref, kb_ref, vb_ref, qm_ref):
    h = _row_rms(x_ref[0], gmix_ref[...]).astype(BF16)
    ang = invf_ref[...] * pos_ref[0].astype(F32)
    cos = jnp.cos(ang)
    sin = jnp.sin(ang)
    q_scale = HEAD_DIM ** -0.5 * LOG2E
    n_blk = TM // KB

    def proj_t(r0, r1):
        return _dot_nt(wt_ref[r0:r1, :], h)

    def gain(i):
        return gains_ref[i]

    def head(t, i):
        return t[i * HEAD_DIM:(i + 1) * HEAD_DIM]

    def store_q(ref, t, n_heads, g, rope):
        for i in range(n_heads):
            y = head(t, i)
            if g is not None:
                y = _head_norm_t(y, g)
            if rope:
                y = _rope_t(y, cos, sin)
            ref[0, i * PAIR:(i + 1) * PAIR, :] = _pair_slot(
                y * q_scale, i % 2 == 1).astype(BF16)

    def store_k(ref, t, n_heads, g, rope):
        for p in range(n_heads // 2):
            ys = []
            for i in (2 * p, 2 * p + 1):
                y = _head_norm_t(head(t, i), g)
                ys.append(_rope_t(y, cos, sin) if rope else y)
            blk = jnp.concatenate(ys, axis=0).T.astype(BF16)
            for j in range(n_blk):
                ref[0, j, :, p * PAIR:(p + 1) * PAIR] = blk[j * KB:(j + 1) * KB]

    def store_v(ref, t):
        tb = t.astype(BF16)
        for j in range(n_blk):
            ref[0, j] = tb[:, j * KB:(j + 1) * KB]

    store_q(qa_ref, proj_t(R_QA, R_KA), N_HEADS_A, gain(0), True)
    store_k(ka_ref, proj_t(R_KA, R_VA), N_HEADS_A, gain(1), True)
    store_v(va_ref, proj_t(R_VA, R_QI))

    t = proj_t(R_QI, R_QB)
    for i in range(IDX_HEADS):
        y = _rope_t(head(t, i), cos, sin) * (IDX_DIM ** -0.5)
        qi_ref[0, i * PAIR:(i + 1) * PAIR, :] = _pair_slot(y, False).astype(BF16)
    ki = _rope_t(_head_norm_t(head(t, IDX_HEADS), gain(2)), cos, sin)
    ki_blk = _pair_slot(ki, False).T.astype(BF16)
    for j in range(n_blk):
        ki_ref[0, j] = ki_blk[j * KB:(j + 1) * KB]
    wi_ref[0] = t[R_WI - R_QI:R_WI - R_QI + WI_ROWS] * (IDX_HEADS ** -0.5)

    store_q(qb_ref, proj_t(R_QB, R_KB), N_HEADS_B, gain(3), False)
    store_k(kb_ref, proj_t(R_KB, R_VB), N_HEADS_B, gain(4), False)
    store_v(vb_ref, proj_t(R_VB, R_QM))
    store_q(qm_ref, proj_t(R_QM, R_END), N_HEADS_M, gain(5), False)


def _proj_call(x, pos3, invf, gmix, wt, layer, gains):
    B, S, _ = x.shape
    nkb = S // KB
    n_blk = TM // KB

    def tok_t(rows):
        return pl.BlockSpec((1, rows, TM), lambda b, i: (b, 0, i))

    def full(shape):
        return pl.BlockSpec(shape, lambda b, i: (0,) * len(shape))

    def krow(width):
        return pl.BlockSpec((1, n_blk, KB, width), lambda b, i: (b, i, 0, 0))

    def vt(rows):
        return pl.BlockSpec((1, n_blk, rows, KB), lambda b, i: (b, i, 0, 0))

    out_shape = (
        jax.ShapeDtypeStruct((B, N_HEADS_A * PAIR, S), BF16),
        jax.ShapeDtypeStruct((B, nkb, KB, WIDTH_A), BF16),
        jax.ShapeDtypeStruct((B, nkb, WIDTH_A, KB), BF16),
        jax.ShapeDtypeStruct((B, IDX_HEADS * PAIR, S), BF16),
        jax.ShapeDtypeStruct((B, nkb, KB, PAIR), BF16),
        jax.ShapeDtypeStruct((B, WI_ROWS, S), F32),
        jax.ShapeDtypeStruct((B, N_HEADS_B * PAIR, S), BF16),
        jax.ShapeDtypeStruct((B, nkb, KB, WIDTH_B), BF16),
        jax.ShapeDtypeStruct((B, nkb, WIDTH_B, KB), BF16),
        jax.ShapeDtypeStruct((B, N_HEADS_M * PAIR, S), BF16),
    )
    out_specs = (
        tok_t(N_HEADS_A * PAIR), krow(WIDTH_A), vt(WIDTH_A),
        tok_t(IDX_HEADS * PAIR), krow(PAIR), tok_t(WI_ROWS),
        tok_t(N_HEADS_B * PAIR), krow(WIDTH_B), vt(WIDTH_B),
        tok_t(N_HEADS_M * PAIR),
    )
    return pl.pallas_call(
        _proj_kernel,
        grid=(B, S // TM),
        in_specs=[
            pl.BlockSpec((1, TM, D_MODEL), lambda b, i: (b, i, 0)),
            pl.BlockSpec((1, 1, TM), lambda b, i: (b, 0, i)),
            full((ROT_HALF, 1)),
            full((1, D_MODEL)),
            pl.BlockSpec((None, R_END, D_MODEL), lambda b, i: (layer, 0, 0)),
            full((6, HEAD_DIM, 1)),
        ],
        out_specs=out_specs,
        out_shape=out_shape,
        compiler_params=pltpu.CompilerParams(
            dimension_semantics=("parallel", "parallel"),
            vmem_limit_bytes=VMEM_LIMIT),
        name="in_proj",
    )(x, pos3, invf, gmix, wt, gains)


def _memkv_kernel(mem_ref, gmem_ref, wt_ref, gk_ref, mk_ref, mv_ref):
    hm = _row_rms(mem_ref[0], gmem_ref[...]).astype(BF16)
    kt = _dot_nt(wt_ref[0:WIDTH_M, :], hm)
    for p in range(N_HEADS_M // 2):
        ys = [_head_norm_t(kt[i * HEAD_DIM:(i + 1) * HEAD_DIM], gk_ref[...])
              for i in (2 * p, 2 * p + 1)]
        mk_ref[0, :, p * PAIR:(p + 1) * PAIR] = (
            jnp.concatenate(ys, axis=0).T.astype(BF16))
    mv_ref[0] = _dot_nt(wt_ref[WIDTH_M:2 * WIDTH_M, :], hm).astype(BF16)


def _memkv_call(mem, gmem, wt, layer, gk):
    B, n_mem, _ = mem.shape
    return pl.pallas_call(
        _memkv_kernel,
        grid=(B,),
        in_specs=[
            pl.BlockSpec((1, n_mem, D_MODEL), lambda b: (b, 0, 0)),
            pl.BlockSpec((1, D_MODEL), lambda b: (0, 0)),
            pl.BlockSpec((None, 2 * WIDTH_M, D_MODEL), lambda b: (layer, 0, 0)),
            pl.BlockSpec((HEAD_DIM, 1), lambda b: (0, 0)),
        ],
        out_specs=(
            pl.BlockSpec((1, n_mem, WIDTH_M), lambda b: (b, 0, 0)),
            pl.BlockSpec((1, WIDTH_M, n_mem), lambda b: (b, 0, 0)),
        ),
        out_shape=(
            jax.ShapeDtypeStruct((B, n_mem, WIDTH_M), BF16),
            jax.ShapeDtypeStruct((B, WIDTH_M, n_mem), BF16),
        ),
        compiler_params=pltpu.CompilerParams(
            dimension_semantics=("parallel",), vmem_limit_bytes=VMEM_LIMIT),
        name="mem_kv",
    )(mem, gmem, wt, gk)


def _bias_kernel(f_ref, out_ref):
    n_keys = BAND_BLOCKS * KB
    width = f_ref.shape[-1]
    base = jnp.broadcast_to(f_ref[0, 0], (n_keys, width))
    rolled = pltpu.roll(base, width - (n_keys - 1), 1, stride=1, stride_axis=0)
    tab = rolled[:, 0:TQ]
    key_c = _chunk_of(lax.broadcasted_iota(I32, (n_keys, TQ), 0))
    qry_c = _chunk_of(lax.broadcasted_iota(I32, (n_keys, TQ), 1)
                      + (BAND_BLOCKS - 1) * KB)
    ok = (key_c <= qry_c) & (key_c >= qry_c - PREV_CHUNKS)
    out_ref[0, 0] = jnp.where(ok, tab * LOG2E, NEG_INF)


def _bias_call(fvec):
    depth, n_heads, _, width = fvec.shape
    n_keys = BAND_BLOCKS * KB
    return pl.pallas_call(
        _bias_kernel,
        grid=(depth, n_heads),
        in_specs=[pl.BlockSpec((1, 1, 1, width), lambda l, h: (l, h, 0, 0))],
        out_specs=pl.BlockSpec((1, 1, n_keys, TQ), lambda l, h: (l, h, 0, 0)),
        out_shape=jax.ShapeDtypeStruct((depth, n_heads, n_keys, TQ), F32),
        compiler_params=pltpu.CompilerParams(
            dimension_semantics=("parallel", "parallel")),
        name="band_bias",
    )(fvec)


def _col_max(x):
    return jnp.max(x, axis=0, keepdims=True)


def _col_sum(x):
    return jnp.sum(x, axis=0, keepdims=True)


def _finish_pair(outs):
    return jnp.concatenate(outs, axis=0).T.astype(BF16)


def _dsa_kernel(qa_ref, ka_ref, va_ref, qi_ref, ki_ref, wi_ref, out_ref,
                keys_ref, planes_ref, mask_ref, acc_ref):
    t = pl.program_id(1)
    nkb = t + 1
    n_kb = keys_ref.shape[0]
    n_pair = nkb // 2
    odd = nkb % 2 == 1
    qry_c = _chunk_of(t * TQ + lax.broadcasted_iota(I32, (KB, TQ), 1))
    key_c0 = _chunk_of(lax.broadcasted_iota(I32, (KB, TQ), 0))

    def score_blocks(kbs):
        dss = [[_dot(ki_ref[0, kb], qi_ref[0, i * PAIR:(i + 1) * PAIR, :])
                for i in range(IDX_HEADS)] for kb in kbs]
        for kb, ds in zip(kbs, dss):
            s = jnp.zeros((KB, TQ), F32)
            for i in range(IDX_HEADS):
                s = s + jnp.maximum(ds[i], 0.0) * wi_ref[0, i:i + 1, :]
            s = jnp.where(s == 0.0, 0.0, s)
            bits = lax.bitcast_convert_type(s, I32)
            key = jnp.where(bits >= 0, bits, bits ^ jnp.int32(0x7FFFFFFF))
            allowed = (key_c0 + kb * (KB // CHUNK)) <= qry_c
            key = jnp.where(allowed, key, jnp.int32(INT_MIN))
            keys_ref[kb] = key
            u = key ^ jnp.int32(INT_MIN)
            cols = _bit_transpose32(
                [u[j * SUBLANES:(j + 1) * SUBLANES] for j in range(KEY_BITS)])
            for b in range(KEY_BITS):
                planes_ref[b, kb] = cols[KEY_BITS - 1 - b]

    def score_pair(j, carry):
        score_blocks([2 * j, 2 * j + 1])
        return carry

    lax.fori_loop(0, n_pair, score_pair, 0)

    @pl.when(odd)
    def _():
        score_blocks([nkb - 1])

    @pl.when(t == 0)
    def _():
        planes_ref[:, 1:] = jnp.zeros_like(planes_ref[:, 1:])

    def col_count(words):
        return _col_sum(_tree_sum([lax.population_count(w) for w in words]))

    def bit(i, carry):
        active, rank, thr_u = carry
        b = KEY_BITS - 1 - i
        planes = planes_ref[b]
        ones = [active[kb] & planes[kb] for kb in range(n_kb)]
        c1 = col_count(ones)
        take = c1 >= rank
        active = tuple(jnp.where(take, ones[kb], active[kb] ^ ones[kb])
                       for kb in range(n_kb))
        rank = jnp.where(take, rank, rank - c1)
        thr_u = jnp.where(take, thr_u | lax.shift_left(jnp.int32(1), b), thr_u)
        return active, rank, thr_u

    active0 = tuple(jnp.full((SUBLANES, TQ), jnp.where(kb < nkb, -1, 0), I32)
                    for kb in range(n_kb))
    active, rank, thr_u = lax.fori_loop(
        0, KEY_BITS, bit,
        (active0, jnp.full((1, TQ), TOPK_MAX, I32), jnp.zeros((1, TQ), I32)))
    thr = thr_u ^ jnp.int32(INT_MIN)
    n_ge = TOPK_MAX - rank + col_count(active)
    tie = (n_ge > TOPK_MAX) & (thr > INT_MIN)
    any_tie = jnp.max(jnp.where(tie, 1.0, 0.0)) > 0.0

    @pl.when(jnp.logical_not(any_tie))
    def _():
        floor = jnp.maximum(thr, jnp.int32(INT_MIN + 1))

        def blk(kb, carry):
            mask_ref[kb] = jnp.where(keys_ref[kb] >= floor, 0.0, MASKED)
            return carry

        lax.fori_loop(0, nkb, blk, 0)

    @pl.when(any_tie)
    def _():
        lower = (lax.broadcasted_iota(I32, (KB, KB), 1)
                 < lax.broadcasted_iota(I32, (KB, KB), 0))
        ltri = jnp.where(lower, 1.0, 0.0).astype(BF16)
        need = rank.astype(F32)

        def blk(kb, seen):
            k = keys_ref[kb]
            eq = jnp.where(k == thr, 1.0, 0.0)
            before = _dot(ltri, eq.astype(BF16)) + seen
            keep = (k > thr) | ((k == thr) & (before < need))
            keep = keep & (k > INT_MIN)
            mask_ref[kb] = jnp.where(keep, 0.0, MASKED)
            return seen + _col_sum(eq)

        lax.fori_loop(0, nkb, blk, jnp.zeros((1, TQ), F32))

    acc_ref[...] = jnp.zeros_like(acc_ref)
    heads = range(N_HEADS_A)

    def attn_blocks(kbs, ms, ls):
        lgs = []
        for kb in kbs:
            mask = mask_ref[kb]
            lgs.append([
                _dot(ka_ref[0, kb, :, (h // 2) * PAIR:(h // 2 + 1) * PAIR],
                     qa_ref[0, h * PAIR:(h + 1) * PAIR, :]) + mask
                for h in heads])
        ms_new = [functools.reduce(
            jnp.maximum, [ms[h]] + [_col_max(lg[h]) for lg in lgs])
            for h in heads]
        alphas = [jnp.exp2(ms[h] - ms_new[h]) for h in heads]
        es = [[jnp.exp2(lg[h] - ms_new[h]) for h in heads] for lg in lgs]
        ls_new = [ls[h] * alphas[h]
                  + functools.reduce(jnp.add, [_col_sum(e[h]) for e in es])
                  for h in heads]
        pvs = [functools.reduce(jnp.add, [
            _dot(va_ref[0, kb, h * HEAD_DIM:(h + 1) * HEAD_DIM, :],
                 e[h].astype(BF16)) for kb, e in zip(kbs, es)])
            for h in heads]
        for h in heads:
            acc_ref[h] = acc_ref[h] * alphas[h] + pvs[h]
        return tuple(ms_new), tuple(ls_new)

    def attn_pair(j, carry):
        return attn_blocks([2 * j, 2 * j + 1], *carry)

    m0 = tuple(jnp.full((1, TQ), MASKED, F32) for _ in heads)
    l0 = tuple(jnp.zeros((1, TQ), F32) for _ in heads)
    ms, ls = lax.fori_loop(0, n_pair, attn_pair, (m0, l0))
    _, ls = lax.cond(odd, lambda: attn_blocks([nkb - 1], ms, ls),
                     lambda: (ms, ls))
    for p in range(N_HEADS_A // 2):
        outs = [acc_ref[h] / ls[h] for h in (2 * p, 2 * p + 1)]
        out_ref[0, :, p * PAIR:(p + 1) * PAIR] = _finish_pair(outs)


def _dsa_call(qa, ka, va, qi, ki, wi):
    B, nkb = ka.shape[0], ka.shape[1]
    S = nkb * KB

    def tile_t(rows):
        return pl.BlockSpec((1, rows, TQ), lambda b, t: (b, 0, t))

    def whole(a):
        return pl.BlockSpec((1,) + a.shape[1:], lambda b, t: (b, 0, 0, 0))

    return pl.pallas_call(
        _dsa_kernel,
        grid=(B, S // TQ),
        in_specs=[tile_t(N_HEADS_A * PAIR), whole(ka), whole(va),
                  tile_t(IDX_HEADS * PAIR), whole(ki), tile_t(WI_ROWS)],
        out_specs=pl.BlockSpec((1, TQ, WIDTH_A), lambda b, t: (b, t, 0)),
        out_shape=jax.ShapeDtypeStruct((B, S, WIDTH_A), BF16),
        scratch_shapes=[
            pltpu.VMEM((nkb, KB, TQ), I32),
            pltpu.VMEM((KEY_BITS, nkb, SUBLANES, TQ), I32),
            pltpu.VMEM((nkb, KB, TQ), F32),
            pltpu.VMEM((N_HEADS_A, HEAD_DIM, TQ), F32),
        ],
        compiler_params=pltpu.CompilerParams(
            dimension_semantics=("parallel", "arbitrary"),
            vmem_limit_bytes=VMEM_LIMIT),
        name="dsa_attn",
    )(qa, ka, va, qi, ki, wi)


def _band_mem_kernel(qb_ref, kb_ref, vb_ref, bias_ref, qm_ref, mk_ref, mv_ref,
                     out_ref):
    t = pl.program_id(1)
    srcs, pens = [], []
    for r in range(BAND_BLOCKS):
        blk = t - (BAND_BLOCKS - 1) + r
        srcs.append(jnp.maximum(blk, 0))
        pens.append(jnp.where(blk >= 0, 0.0, NEG_INF))

    def head_rows(h):
        return slice(h * HEAD_DIM, (h + 1) * HEAD_DIM)

    def pair_cols(h):
        return slice((h // 2) * PAIR, (h // 2 + 1) * PAIR)

    band_lg = [[_dot(kb_ref[0, srcs[r], :, pair_cols(h)],
                     qb_ref[0, h * PAIR:(h + 1) * PAIR, :])
                + bias_ref[0, h, r * KB:(r + 1) * KB, :] + pens[r]
                for r in range(BAND_BLOCKS)] for h in range(N_HEADS_B)]
    mem_lg = [_dot(mk_ref[0, :, pair_cols(h)],
                   qm_ref[0, h * PAIR:(h + 1) * PAIR, :])
              for h in range(N_HEADS_M)]

    band_e, band_l = [], []
    for lgs in band_lg:
        m = functools.reduce(jnp.maximum, [_col_max(lg) for lg in lgs])
        es = [jnp.exp2(lg - m) for lg in lgs]
        band_e.append([e.astype(BF16) for e in es])
        band_l.append(functools.reduce(jnp.add, [_col_sum(e) for e in es]))
    mem_e, mem_l = [], []
    for lg in mem_lg:
        e = jnp.exp2(lg - _col_max(lg))
        mem_e.append(e.astype(BF16))
        mem_l.append(_col_sum(e))

    band_o = [functools.reduce(jnp.add, [
        _dot(vb_ref[0, srcs[r], head_rows(h), :], band_e[h][r])
        for r in range(BAND_BLOCKS)]) / band_l[h] for h in range(N_HEADS_B)]
    mem_o = [_dot(mv_ref[0, head_rows(h), :], mem_e[h]) / mem_l[h]
             for h in range(N_HEADS_M)]

    for p in range(N_HEADS_B // 2):
        out_ref[0, :, p * PAIR:(p + 1) * PAIR] = _finish_pair(
            band_o[2 * p:2 * p + 2])
    for p in range(N_HEADS_M // 2):
        out_ref[0, :, WIDTH_B + p * PAIR:WIDTH_B + (p + 1) * PAIR] = (
            _finish_pair(mem_o[2 * p:2 * p + 2]))


def _band_mem_call(qb, kb, vb, bias, layer, qm, mk, mv):
    B, nkb = kb.shape[0], kb.shape[1]
    S = nkb * KB
    n_mem = mk.shape[1]

    def tile_t(rows):
        return pl.BlockSpec((1, rows, TQ), lambda b, t: (b, 0, t))

    def whole(a):
        return pl.BlockSpec((1,) + a.shape[1:], lambda b, t: (b, 0, 0, 0))

    return pl.pallas_call(
        _band_mem_kernel,
        grid=(B, S // TQ),
        in_specs=[
            tile_t(N_HEADS_B * PAIR), whole(kb), whole(vb),
            pl.BlockSpec((1,) + bias.shape[1:], lambda b, t: (layer, 0, 0, 0)),
            tile_t(N_HEADS_M * PAIR),
            pl.BlockSpec((1, n_mem, WIDTH_M), lambda b, t: (b, 0, 0)),
            pl.BlockSpec((1, WIDTH_M, n_mem), lambda b, t: (b, 0, 0)),
        ],
        out_specs=pl.BlockSpec((1, TQ, WIDTH_B + WIDTH_M), lambda b, t: (b, t, 0)),
        out_shape=jax.ShapeDtypeStruct((B, S, WIDTH_B + WIDTH_M), BF16),
        compiler_params=pltpu.CompilerParams(
            dimension_semantics=("parallel", "parallel"),
            vmem_limit_bytes=VMEM_LIMIT),
        name="band_mem_attn",
    )(qb, kb, vb, bias, qm, mk, mv)


FF_CHUNK = D_FF // 2


def _out_ffn_kernel(x_ref, oa_ref, obm_ref, wo_ref, gffn_ref, wgu_ref, wd_ref,
                    out_ref):
    attn = jnp.concatenate([oa_ref[...], obm_ref[...]], axis=1)
    x1 = x_ref[...] + _dot(attn, wo_ref[...])
    h = _row_rms(x1, gffn_ref[...]).astype(BF16)
    acc = x1
    for c in range(D_FF // FF_CHUNK):
        c0 = c * FF_CHUNK
        gate = _dot(h, wgu_ref[:, c0:c0 + FF_CHUNK])
        up = _dot(h, wgu_ref[:, D_FF + c0:D_FF + c0 + FF_CHUNK])
        act = gate * (1.0 / (1.0 + jnp.exp(-gate))) * up
        acc = acc + _dot(act.astype(BF16), wd_ref[c0:c0 + FF_CHUNK, :])
    out_ref[...] = acc


def _out_ffn_call(x2, oa2, obm2, wo, gffn, wgu, wd, layer):
    n = x2.shape[0]

    def rows(width):
        return pl.BlockSpec((TM, width), lambda i: (i, 0))

    def resident(a):
        return pl.BlockSpec((None,) + a.shape[1:], lambda i: (layer, 0, 0),
                            pipeline_mode=pl.Buffered(1))

    return pl.pallas_call(
        _out_ffn_kernel,
        grid=(n // TM,),
        in_specs=[rows(D_MODEL), rows(WIDTH_A), rows(WIDTH_B + WIDTH_M),
                  resident(wo), resident(gffn), resident(wgu), resident(wd)],
        out_specs=rows(D_MODEL),
        out_shape=jax.ShapeDtypeStruct((n, D_MODEL), F32),
        compiler_params=pltpu.CompilerParams(
            dimension_semantics=("parallel",), vmem_limit_bytes=VMEM_LIMIT),
        name="out_ffn",
    )(x2, oa2, obm2, wo, gffn, wgu, wd)


def kernel(x, mem, positions, g_mix, w_in, g_q_a, g_k_a, g_k_idx, g_q_b, g_k_b,
           rel_bias, g_q_m, g_k_m, g_mem, w_mem_kv, w_out, g_ffn, w_gate_up,
           w_down):
    B, S, D = x.shape
    depth = w_in.shape[0]

    w_in_b = w_in.astype(BF16)
    n_wi = R_KI + IDX_DIM + IDX_HEADS
    w_in_t = jnp.swapaxes(jnp.concatenate(
        [w_in_b[..., :n_wi],
         jnp.zeros((depth, D, WI_PAD - IDX_HEADS), BF16),
         w_in_b[..., n_wi:]], axis=-1), 1, 2)
    w_mem_t = jnp.swapaxes(w_mem_kv.astype(BF16), 1, 2)
    w_out_b = w_out.astype(BF16)
    w_gu_b = w_gate_up.astype(BF16)
    w_down_b = w_down.astype(BF16)
    gains = jnp.stack(
        [g_q_a, g_k_a, g_k_idx, g_q_b, g_k_b, g_q_m], axis=1)[..., None]

    inv_freq = jnp.power(
        ROPE_THETA, -jnp.arange(ROT_HALF, dtype=F32) / ROT_HALF)[:, None]
    pos3 = positions.reshape(B, 1, S)

    n_f = 2 * REL_CLIP
    fvec = jnp.concatenate(
        [rel_bias[..., 1:], jnp.broadcast_to(rel_bias[..., -1:],
                                             rel_bias.shape[:-1] + (n_f,))],
        axis=-1)[:, :, None, :]
    bias = _bias_call(fvec)

    for l in range(depth):
        qa, ka, va, qi, ki, wi, qb, kb, vb, qm = _proj_call(
            x, pos3, inv_freq, g_mix[l][None], w_in_t, l, gains[l])
        mk, mv = _memkv_call(mem, g_mem[l][None], w_mem_t, l, g_k_m[l][:, None])
        oa = _dsa_call(qa, ka, va, qi, ki, wi)
        obm = _band_mem_call(qb, kb, vb, bias, l, qm, mk, mv)
        x = _out_ffn_call(
            x.reshape(B * S, D), oa.reshape(B * S, WIDTH_A),
            obm.reshape(B * S, WIDTH_B + WIDTH_M), w_out_b, g_ffn[:, None],
            w_gu_b, w_down_b, l).reshape(B, S, D)
    return x
```

```python
import functools

import jax
import jax.numpy as jnp
from jax import lax
from jax.experimental import pallas as pl
from jax.experimental.pallas import tpu as pltpu

F32 = jnp.float32
BF16 = jnp.bfloat16
I32 = jnp.int32

D_MODEL = 1024
HEAD_DIM = 64
N_HEADS_A = 6
N_HEADS_B = 6
N_HEADS_M = 4
WIDTH_A = N_HEADS_A * HEAD_DIM
WIDTH_B = N_HEADS_B * HEAD_DIM
WIDTH_M = N_HEADS_M * HEAD_DIM
IDX_HEADS = 4
IDX_DIM = 64
ROT_DIM = HEAD_DIM // 4
ROT_HALF = ROT_DIM // 2
ROPE_THETA = 500000.0
CHUNK = 64
PREV_CHUNKS = 8
REL_CLIP = 256
TOPK_MAX = 256
D_FF = 2816
EPS = 1e-6
LOG2E = 1.4426950408889634

LANES = 128
SUBLANES = 8
PAIR = 2 * HEAD_DIM
TM = 1024
TF = 512
TQ = 256
KB = 256
BAND_BLOCKS = 3
WI_ROWS = 8
WI_PAD = 16
INT_MIN = -(2 ** 31)
KEY_BITS = 32
NEG_INF = float("-inf")
MASKED = -1e30
VMEM_LIMIT = 56 * 1024 * 1024

R_QA = 0
R_KA = R_QA + WIDTH_A
R_VA = R_KA + WIDTH_A
R_QI = R_VA + WIDTH_A
R_KI = R_QI + IDX_HEADS * IDX_DIM
R_WI = R_KI + IDX_DIM
R_QB = R_WI + WI_PAD
R_KB = R_QB + WIDTH_B
R_VB = R_KB + WIDTH_B
R_QM = R_VB + WIDTH_B
R_END = R_QM + WIDTH_M

NT_DIMS = (((1,), (1,)), ((), ()))


def _dot(a, b):
    return jnp.dot(a, b, preferred_element_type=F32)


def _dot_nt(a, b):
    return lax.dot_general(a, b, NT_DIMS, preferred_element_type=F32)


def _chunk_of(pos):
    return jnp.right_shift(pos, CHUNK.bit_length() - 1)


def _tree_sum(xs):
    while len(xs) > 1:
        xs = [a + b for a, b in zip(xs[0::2], xs[1::2])] + xs[len(xs) & ~1:]
    return xs[0]


def _as_i32(v):
    return v - (1 << 32) if v >= (1 << 31) else v


def _bit_transpose32(rows):
    rows = list(rows)
    j, m = 16, 0x0000FFFF
    while j:
        k = 0
        while k < 32:
            t = ((rows[k] ^ lax.shift_right_logical(rows[k + j], jnp.int32(j)))
                 & jnp.int32(_as_i32(m)))
            rows[k] = rows[k] ^ t
            rows[k + j] = rows[k + j] ^ lax.shift_left(t, jnp.int32(j))
            k = (k + j + 1) & ~j
        j >>= 1
        if j:
            m = (m ^ (m << j)) & 0xFFFFFFFF
    return rows


def _row_rms(x, g):
    ms = jnp.mean(x * x, axis=-1, keepdims=True)
    return x * lax.rsqrt(ms + EPS) * g


def _head_norm_t(blk, g_col):
    ms = jnp.mean(blk * blk, axis=0, keepdims=True)
    return blk * lax.rsqrt(ms + EPS) * g_col


def _rope_t(y, cos, sin):
    y1 = y[0:ROT_HALF]
    y2 = y[ROT_HALF:ROT_DIM]
    return jnp.concatenate(
        [y1 * cos - y2 * sin, y2 * cos + y1 * sin, y[ROT_DIM:]], axis=0)


def _pair_slot(y, odd):
    z = jnp.zeros_like(y)
    return jnp.concatenate([z, y] if odd else [y, z], axis=0)


def _proj_kernel(x_ref, pos_ref, invf_ref, gmix_ref, wt_ref, gains_ref,
                 qa_ref, ka_ref, va_ref, qi_ref, ki_ref, wi_ref,
                 qb_ref, kb_ref, vb_ref, qm_ref):
    h = _row_rms(x_ref[0], gmix_ref[...]).astype(BF16)
    ang = invf_ref[...] * pos_ref[0].astype(F32)
    cos = jnp.cos(ang)
    sin = jnp.sin(ang)
    q_scale = HEAD_DIM ** -0.5 * LOG2E
    n_blk = TM // KB

    def proj_t(r0, r1):
        return _dot_nt(wt_ref[r0:r1, :], h)

    def gain(i):
        return gains_ref[i]

    def head(t, i):
        return t[i * HEAD_DIM:(i + 1) * HEAD_DIM]

    def store_q(ref, t, n_heads, g, rope):
        for i in range(n_heads):
            y = head(t, i)
            if g is not None:
                y = _head_norm_t(y, g)
            if rope:
                y = _rope_t(y, cos, sin)
            ref[0, i * PAIR:(i + 1) * PAIR, :] = _pair_slot(
                y * q_scale, i % 2 == 1).astype(BF16)

    def store_k(ref, t, n_heads, g, rope):
        for p in range(n_heads // 2):
            ys = []
            for i in (2 * p, 2 * p + 1):
                y = _head_norm_t(head(t, i), g)
                ys.append(_rope_t(y, cos, sin) if rope else y)
            blk = jnp.concatenate(ys, axis=0).T.astype(BF16)
            for j in range(n_blk):
                ref[0, j, :, p * PAIR:(p + 1) * PAIR] = blk[j * KB:(j + 1) * KB]

    def store_v(ref, t):
        tb = t.astype(BF16)
        for j in range(n_blk):
            ref[0, j] = tb[:, j * KB:(j + 1) * KB]

    store_q(qa_ref, proj_t(R_QA, R_KA), N_HEADS_A, gain(0), True)
    store_k(ka_ref, proj_t(R_KA, R_VA), N_HEADS_A, gain(1), True)
    store_v(va_ref, proj_t(R_VA, R_QI))

    t = proj_t(R_QI, R_QB)
    for i in range(IDX_HEADS):
        y = _rope_t(head(t, i), cos, sin) * (IDX_DIM ** -0.5)
        qi_ref[0, i * PAIR:(i + 1) * PAIR, :] = _pair_slot(y, False).astype(BF16)
    ki = _rope_t(_head_norm_t(head(t, IDX_HEADS), gain(2)), cos, sin)
    ki_blk = _pair_slot(ki, False).T.astype(BF16)
    for j in range(n_blk):
        ki_ref[0, j] = ki_blk[j * KB:(j + 1) * KB]
    wi_ref[0] = t[R_WI - R_QI:R_WI - R_QI + WI_ROWS] * (IDX_HEADS ** -0.5)

    store_q(qb_ref, proj_t(R_QB, R_KB), N_HEADS_B, gain(3), False)
    store_k(kb_ref, proj_t(R_KB, R_VB), N_HEADS_B, gain(4), False)
    store_v(vb_ref, proj_t(R_VB, R_QM))
    store_q(qm_ref, proj_t(R_QM, R_END), N_HEADS_M, gain(5), False)


def _proj_call(x, pos3, invf, gmix, wt, layer, gains):
    B, S, _ = x.shape
    nkb = S // KB
    n_blk = TM // KB

    def tok_t(rows):
        return pl.BlockSpec((1, rows, TM), lambda b, i: (b, 0, i))

    def full(shape):
        return pl.BlockSpec(shape, lambda b, i: (0,) * len(shape))

    def krow(width):
        return pl.BlockSpec((1, n_blk, KB, width), lambda b, i: (b, i, 0, 0))

    def vt(rows):
        return pl.BlockSpec((1, n_blk, rows, KB), lambda b, i: (b, i, 0, 0))

    out_shape = (
        jax.ShapeDtypeStruct((B, N_HEADS_A * PAIR, S), BF16),
        jax.ShapeDtypeStruct((B, nkb, KB, WIDTH_A), BF16),
        jax.ShapeDtypeStruct((B, nkb, WIDTH_A, KB), BF16),
        jax.ShapeDtypeStruct((B, IDX_HEADS * PAIR, S), BF16),
        jax.ShapeDtypeStruct((B, nkb, KB, PAIR), BF16),
        jax.ShapeDtypeStruct((B, WI_ROWS, S), F32),
        jax.ShapeDtypeStruct((B, N_HEADS_B * PAIR, S), BF16),
        jax.ShapeDtypeStruct((B, nkb, KB, WIDTH_B), BF16),
        jax.ShapeDtypeStruct((B, nkb, WIDTH_B, KB), BF16),
        jax.ShapeDtypeStruct((B, N_HEADS_M * PAIR, S), BF16),
    )
    out_specs = (
        tok_t(N_HEADS_A * PAIR), krow(WIDTH_A), vt(WIDTH_A),
        tok_t(IDX_HEADS * PAIR), krow(PAIR), tok_t(WI_ROWS),
        tok_t(N_HEADS_B * PAIR), krow(WIDTH_B), vt(WIDTH_B),
        tok_t(N_HEADS_M * PAIR),
    )
    return pl.pallas_call(
        _proj_kernel,
        grid=(B, S // TM),
        in_specs=[
            pl.BlockSpec((1, TM, D_MODEL), lambda b, i: (b, i, 0)),
            pl.BlockSpec((1, 1, TM), lambda b, i: (b, 0, i)),
            full((ROT_HALF, 1)),
            full((1, D_MODEL)),
            pl.BlockSpec((None, R_END, D_MODEL), lambda b, i: (layer, 0, 0)),
            full((6, HEAD_DIM, 1)),
        ],
        out_specs=out_specs,
        out_shape=out_shape,
        compiler_params=pltpu.CompilerParams(
            dimension_semantics=("parallel", "parallel"),
            vmem_limit_bytes=VMEM_LIMIT),
        name="in_proj",
    )(x, pos3, invf, gmix, wt, gains)


def _memkv_kernel(mem_ref, gmem_ref, wt_ref, gk_ref, mk_ref, mv_ref):
    hm = _row_rms(mem_ref[0], gmem_ref[...]).astype(BF16)
    kt = _dot_nt(wt_ref[0:WIDTH_M, :], hm)
    for p in range(N_HEADS_M // 2):
        ys = [_head_norm_t(kt[i * HEAD_DIM:(i + 1) * HEAD_DIM], gk_ref[...])
              for i in (2 * p, 2 * p + 1)]
        mk_ref[0, :, p * PAIR:(p + 1) * PAIR] = (
            jnp.concatenate(ys, axis=0).T.astype(BF16))
    mv_ref[0] = _dot_nt(wt_ref[WIDTH_M:2 * WIDTH_M, :], hm).astype(BF16)


def _memkv_call(mem, gmem, wt, layer, gk):
    B, n_mem, _ = mem.shape
    return pl.pallas_call(
        _memkv_kernel,
        grid=(B,),
        in_specs=[
            pl.BlockSpec((1, n_mem, D_MODEL), lambda b: (b, 0, 0)),
            pl.BlockSpec((1, D_MODEL), lambda b: (0, 0)),
            pl.BlockSpec((None, 2 * WIDTH_M, D_MODEL), lambda b: (layer, 0, 0)),
            pl.BlockSpec((HEAD_DIM, 1), lambda b: (0, 0)),
        ],
        out_specs=(
            pl.BlockSpec((1, n_mem, WIDTH_M), lambda b: (b, 0, 0)),
            pl.BlockSpec((1, WIDTH_M, n_mem), lambda b: (b, 0, 0)),
        ),
        out_shape=(
            jax.ShapeDtypeStruct((B, n_mem, WIDTH_M), BF16),
            jax.ShapeDtypeStruct((B, WIDTH_M, n_mem), BF16),
        ),
        compiler_params=pltpu.CompilerParams(
            dimension_semantics=("parallel",), vmem_limit_bytes=VMEM_LIMIT),
        name="mem_kv",
    )(mem, gmem, wt, gk)


def _bias_kernel(f_ref, out_ref):
    n_keys = BAND_BLOCKS * KB
    width = f_ref.shape[-1]
    base = jnp.broadcast_to(f_ref[0, 0], (n_keys, width))
    rolled = pltpu.roll(base, width - (n_keys - 1), 1, stride=1, stride_axis=0)
    tab = rolled[:, 0:TQ]
    key_c = _chunk_of(lax.broadcasted_iota(I32, (n_keys, TQ), 0))
    qry_c = _chunk_of(lax.broadcasted_iota(I32, (n_keys, TQ), 1)
                      + (BAND_BLOCKS - 1) * KB)
    ok = (key_c <= qry_c) & (key_c >= qry_c - PREV_CHUNKS)
    out_ref[0, 0] = jnp.where(ok, tab * LOG2E, NEG_INF)


def _bias_call(fvec):
    depth, n_heads, _, width = fvec.shape
    n_keys = BAND_BLOCKS * KB
    return pl.pallas_call(
        _bias_kernel,
        grid=(depth, n_heads),
        in_specs=[pl.BlockSpec((1, 1, 1, width), lambda l, h: (l, h, 0, 0))],
        out_specs=pl.BlockSpec((1, 1, n_keys, TQ), lambda l, h: (l, h, 0, 0)),
        out_shape=jax.ShapeDtypeStruct((depth, n_heads, n_keys, TQ), F32),
        compiler_params=pltpu.CompilerParams(
            dimension_semantics=("parallel", "parallel")),
        name="band_bias",
    )(fvec)


def _col_max(x):
    return jnp.max(x, axis=0, keepdims=True)


def _col_sum(x):
    return jnp.sum(x, axis=0, keepdims=True)


def _finish_pair(outs):
    return jnp.concatenate(outs, axis=0).T.astype(BF16)


def _dsa_kernel(qa_ref, ka_ref, va_ref, qi_ref, ki_ref, wi_ref, out_ref,
                keys_ref, planes_ref, mask_ref, acc_ref):
    t = pl.program_id(1)
    nkb = t + 1
    n_kb = keys_ref.shape[0]
    n_pair = nkb // 2
    odd = nkb % 2 == 1
    qry_c = _chunk_of(t * TQ + lax.broadcasted_iota(I32, (KB, TQ), 1))
    key_c0 = _chunk_of(lax.broadcasted_iota(I32, (KB, TQ), 0))

    def score_blocks(kbs):
        dss = [[_dot(ki_ref[0, kb], qi_ref[0, i * PAIR:(i + 1) * PAIR, :])
                for i in range(IDX_HEADS)] for kb in kbs]
        for kb, ds in zip(kbs, dss):
            s = jnp.zeros((KB, TQ), F32)
            for i in range(IDX_HEADS):
                s = s + jnp.maximum(ds[i], 0.0) * wi_ref[0, i:i + 1, :]
            s = jnp.where(s == 0.0, 0.0, s)
            bits = lax.bitcast_convert_type(s, I32)
            key = jnp.where(bits >= 0, bits, bits ^ jnp.int32(0x7FFFFFFF))
            allowed = (key_c0 + kb * (KB // CHUNK)) <= qry_c
            key = jnp.where(allowed, key, jnp.int32(INT_MIN))
            keys_ref[kb] = key
            u = key ^ jnp.int32(INT_MIN)
            cols = _bit_transpose32(
                [u[j * SUBLANES:(j + 1) * SUBLANES] for j in range(KEY_BITS)])
            for b in range(KEY_BITS):
                planes_ref[b, kb] = cols[KEY_BITS - 1 - b]

    def score_pair(j, carry):
        score_blocks([2 * j, 2 * j + 1])
        return carry

    lax.fori_loop(0, n_pair, score_pair, 0)

    @pl.when(odd)
    def _():
        score_blocks([nkb - 1])

    @pl.when(t == 0)
    def _():
        planes_ref[:, 1:] = jnp.zeros_like(planes_ref[:, 1:])

    def col_count(words):
        return _col_sum(_tree_sum([lax.population_count(w) for w in words]))

    def bit(i, carry):
        active, rank, thr_u = carry
        b = KEY_BITS - 1 - i
        planes = planes_ref[b]
        ones = [active[kb] & planes[kb] for kb in range(n_kb)]
        c1 = col_count(ones)
        take = c1 >= rank
        active = tuple(jnp.where(take, ones[kb], active[kb] ^ ones[kb])
                       for kb in range(n_kb))
        rank = jnp.where(take, rank, rank - c1)
        thr_u = jnp.where(take, thr_u | lax.shift_left(jnp.int32(1), b), thr_u)
        return active, rank, thr_u

    active0 = tuple(jnp.full((SUBLANES, TQ), jnp.where(kb < nkb, -1, 0), I32)
                    for kb in range(n_kb))
    active, rank, thr_u = lax.fori_loop(
        0, KEY_BITS, bit,
        (active0, jnp.full((1, TQ), TOPK_MAX, I32), jnp.zeros((1, TQ), I32)))
    thr = thr_u ^ jnp.int32(INT_MIN)
    n_ge = TOPK_MAX - rank + col_count(active)
    tie = (n_ge > TOPK_MAX) & (thr > INT_MIN)
    any_tie = jnp.max(jnp.where(tie, 1.0, 0.0)) > 0.0

    @pl.when(jnp.logical_not(any_tie))
    def _():
        floor = jnp.maximum(thr, jnp.int32(INT_MIN + 1))

        def blk(kb, carry):
            mask_ref[kb] = jnp.where(keys_ref[kb] >= floor, 0.0, MASKED)
            return carry

        lax.fori_loop(0, nkb, blk, 0)

    @pl.when(any_tie)
    def _():
        lower = (lax.broadcasted_iota(I32, (KB, KB), 1)
                 < lax.broadcasted_iota(I32, (KB, KB), 0))
        ltri = jnp.where(lower, 1.0, 0.0).astype(BF16)
        need = rank.astype(F32)

        def blk(kb, seen):
            k = keys_ref[kb]
            eq = jnp.where(k == thr, 1.0, 0.0)
            before = _dot(ltri, eq.astype(BF16)) + seen
            keep = (k > thr) | ((k == thr) & (before < need))
            keep = keep & (k > INT_MIN)
            mask_ref[kb] = jnp.where(keep, 0.0, MASKED)
            return seen + _col_sum(eq)

        lax.fori_loop(0, nkb, blk, jnp.zeros((1, TQ), F32))

    acc_ref[...] = jnp.zeros_like(acc_ref)
    heads = range(N_HEADS_A)

    def attn_blocks(kbs, ms, ls):
        lgs = []
        for kb in kbs:
            mask = mask_ref[kb]
            lgs.append([
                _dot(ka_ref[0, kb, :, (h // 2) * PAIR:(h // 2 + 1) * PAIR],
                     qa_ref[0, h * PAIR:(h + 1) * PAIR, :]) + mask
                for h in heads])
        ms_new = [functools.reduce(
            jnp.maximum, [ms[h]] + [_col_max(lg[h]) for lg in lgs])
            for h in heads]
        alphas = [jnp.exp2(ms[h] - ms_new[h]) for h in heads]
        es = [[jnp.exp2(lg[h] - ms_new[h]) for h in heads] for lg in lgs]
        ls_new = [ls[h] * alphas[h]
                  + functools.reduce(jnp.add, [_col_sum(e[h]) for e in es])
                  for h in heads]
        pvs = [functools.reduce(jnp.add, [
            _dot(va_ref[0, kb, h * HEAD_DIM:(h + 1) * HEAD_DIM, :],
                 e[h].astype(BF16)) for kb, e in zip(kbs, es)])
            for h in heads]
        for h in heads:
            acc_ref[h] = acc_ref[h] * alphas[h] + pvs[h]
        return tuple(ms_new), tuple(ls_new)

    def attn_pair(j, carry):
        return attn_blocks([2 * j, 2 * j + 1], *carry)

    m0 = tuple(jnp.full((1, TQ), MASKED, F32) for _ in heads)
    l0 = tuple(jnp.zeros((1, TQ), F32) for _ in heads)
    ms, ls = lax.fori_loop(0, n_pair, attn_pair, (m0, l0))
    _, ls = lax.cond(odd, lambda: attn_blocks([nkb - 1], ms, ls),
                     lambda: (ms, ls))
    for p in range(N_HEADS_A // 2):
        outs = [acc_ref[h] / ls[h] for h in (2 * p, 2 * p + 1)]
        out_ref[0, :, p * PAIR:(p + 1) * PAIR] = _finish_pair(outs)


def _dsa_call(qa, ka, va, qi, ki, wi):
    B, nkb = ka.shape[0], ka.shape[1]
    S = nkb * KB

    def tile_t(rows):
        return pl.BlockSpec((1, rows, TQ), lambda b, t: (b, 0, t))

    def whole(a):
        return pl.BlockSpec((1,) + a.shape[1:], lambda b, t: (b, 0, 0, 0))

    return pl.pallas_call(
        _dsa_kernel,
        grid=(B, S // TQ),
        in_specs=[tile_t(N_HEADS_A * PAIR), whole(ka), whole(va),
                  tile_t(IDX_HEADS * PAIR), whole(ki), tile_t(WI_ROWS)],
        out_specs=pl.BlockSpec((1, TQ, WIDTH_A), lambda b, t: (b, t, 0)),
        out_shape=jax.ShapeDtypeStruct((B, S, WIDTH_A), BF16),
        scratch_shapes=[
            pltpu.VMEM((nkb, KB, TQ), I32),
            pltpu.VMEM((KEY_BITS, nkb, SUBLANES, TQ), I32),
            pltpu.VMEM((nkb, KB, TQ), F32),
            pltpu.VMEM((N_HEADS_A, HEAD_DIM, TQ), F32),
        ],
        compiler_params=pltpu.CompilerParams(
            dimension_semantics=("parallel", "arbitrary"),
            vmem_limit_bytes=VMEM_LIMIT),
        name="dsa_attn",
    )(qa, ka, va, qi, ki, wi)


def _band_mem_kernel(qb_ref, kb_ref, vb_ref, bias_ref, qm_ref, mk_ref, mv_ref,
                     out_ref):
    t = pl.program_id(1)
    srcs, pens = [], []
    for r in range(BAND_BLOCKS):
        blk = t - (BAND_BLOCKS - 1) + r
        srcs.append(jnp.maximum(blk, 0))
        pens.append(jnp.where(blk >= 0, 0.0, NEG_INF))

    def head_rows(h):
        return slice(h * HEAD_DIM, (h + 1) * HEAD_DIM)

    def pair_cols(h):
        return slice((h // 2) * PAIR, (h // 2 + 1) * PAIR)

    band_lg = [[_dot(kb_ref[0, srcs[r], :, pair_cols(h)],
                     qb_ref[0, h * PAIR:(h + 1) * PAIR, :])
                + bias_ref[0, h, r * KB:(r + 1) * KB, :] + pens[r]
                for r in range(BAND_BLOCKS)] for h in range(N_HEADS_B)]
    mem_lg = [_dot(mk_ref[0, :, pair_cols(h)],
                   qm_ref[0, h * PAIR:(h + 1) * PAIR, :])
              for h in range(N_HEADS_M)]

    band_e, band_l = [], []
    for lgs in band_lg:
        m = functools.reduce(jnp.maximum, [_col_max(lg) for lg in lgs])
        es = [jnp.exp2(lg - m) for lg in lgs]
        band_e.append([e.astype(BF16) for e in es])
        band_l.append(functools.reduce(jnp.add, [_col_sum(e) for e in es]))
    mem_e, mem_l = [], []
    for lg in mem_lg:
        e = jnp.exp2(lg - _col_max(lg))
        mem_e.append(e.astype(BF16))
        mem_l.append(_col_sum(e))

    band_o = [functools.reduce(jnp.add, [
        _dot(vb_ref[0, srcs[r], head_rows(h), :], band_e[h][r])
        for r in range(BAND_BLOCKS)]) / band_l[h] for h in range(N_HEADS_B)]
    mem_o = [_dot(mv_ref[0, head_rows(h), :], mem_e[h]) / mem_l[h]
             for h in range(N_HEADS_M)]

    for p in range(N_HEADS_B // 2):
        out_ref[0, :, p * PAIR:(p + 1) * PAIR] = _finish_pair(
            band_o[2 * p:2 * p + 2])
    for p in range(N_HEADS_M // 2):
        out_ref[0, :, WIDTH_B + p * PAIR:WIDTH_B + (p + 1) * PAIR] = (
            _finish_pair(mem_o[2 * p:2 * p + 2]))


def _band_mem_call(qb, kb, vb, bias, layer, qm, mk, mv):
    B, nkb = kb.shape[0], kb.shape[1]
    S = nkb * KB
    n_mem = mk.shape[1]

    def tile_t(rows):
        return pl.BlockSpec((1, rows, TQ), lambda b, t: (b, 0, t))

    def whole(a):
        return pl.BlockSpec((1,) + a.shape[1:], lambda b, t: (b, 0, 0, 0))

    return pl.pallas_call(
        _band_mem_kernel,
        grid=(B, S // TQ),
        in_specs=[
            tile_t(N_HEADS_B * PAIR), whole(kb), whole(vb),
            pl.BlockSpec((1,) + bias.shape[1:], lambda b, t: (layer, 0, 0, 0)),
            tile_t(N_HEADS_M * PAIR),
            pl.BlockSpec((1, n_mem, WIDTH_M), lambda b, t: (b, 0, 0)),
            pl.BlockSpec((1, WIDTH_M, n_mem), lambda b, t: (b, 0, 0)),
        ],
        out_specs=pl.BlockSpec((1, TQ, WIDTH_B + WIDTH_M), lambda b, t: (b, t, 0)),
        out_shape=jax.ShapeDtypeStruct((B, S, WIDTH_B + WIDTH_M), BF16),
        compiler_params=pltpu.CompilerParams(
            dimension_semantics=("parallel", "parallel"),
            vmem_limit_bytes=VMEM_LIMIT),
        name="band_mem_attn",
    )(qb, kb, vb, bias, qm, mk, mv)


FF_CHUNK = 256


def _out_ffn_kernel(x_ref, oa_ref, obm_ref, wo_ref, gffn_ref, wgu_ref, wd_ref,
                    out_ref):
    attn = jnp.concatenate([oa_ref[...], obm_ref[...]], axis=1)
    x1 = x_ref[...] + _dot(attn, wo_ref[...])
    h = _row_rms(x1, gffn_ref[...]).astype(BF16)
    acc = x1
    for c in range(D_FF // FF_CHUNK):
        c0 = c * FF_CHUNK
        gate = _dot(h, wgu_ref[:, c0:c0 + FF_CHUNK])
        up = _dot(h, wgu_ref[:, D_FF + c0:D_FF + c0 + FF_CHUNK])
        act = gate * (1.0 / (1.0 + jnp.exp(-gate))) * up
        acc = acc + _dot(act.astype(BF16), wd_ref[c0:c0 + FF_CHUNK, :])
    out_ref[...] = acc


def _out_ffn_call(x2, oa2, obm2, wo, gffn, wgu, wd, layer):
    n = x2.shape[0]

    def rows(width):
        return pl.BlockSpec((TF, width), lambda i: (i, 0))

    def resident(a):
        return pl.BlockSpec((None,) + a.shape[1:], lambda i: (layer, 0, 0),
                            pipeline_mode=pl.Buffered(1))

    return pl.pallas_call(
        _out_ffn_kernel,
        grid=(n // TF,),
        in_specs=[rows(D_MODEL), rows(WIDTH_A), rows(WIDTH_B + WIDTH_M),
                  resident(wo), resident(gffn), resident(wgu), resident(wd)],
        out_specs=rows(D_MODEL),
        out_shape=jax.ShapeDtypeStruct((n, D_MODEL), F32),
        compiler_params=pltpu.CompilerParams(
            dimension_semantics=("parallel",), vmem_limit_bytes=VMEM_LIMIT),
        name="out_ffn",
    )(x2, oa2, obm2, wo, gffn, wgu, wd)


def kernel(x, mem, positions, g_mix, w_in, g_q_a, g_k_a, g_k_idx, g_q_b, g_k_b,
           rel_bias, g_q_m, g_k_m, g_mem, w_mem_kv, w_out, g_ffn, w_gate_up,
           w_down):
    B, S, D = x.shape
    depth = w_in.shape[0]

    w_in_b = lax.optimization_barrier(w_in.astype(BF16))
    n_wi = R_KI + IDX_DIM + IDX_HEADS
    w_in_p = lax.optimization_barrier(jnp.concatenate(
        [w_in_b[..., :n_wi],
         jnp.zeros((depth, D, WI_PAD - IDX_HEADS), BF16),
         w_in_b[..., n_wi:]], axis=-1))
    w_in_t = jnp.swapaxes(w_in_p, 1, 2)
    w_mem_t = jnp.swapaxes(w_mem_kv.astype(BF16), 1, 2)
    w_out_b = w_out.astype(BF16)
    w_gu_b = w_gate_up.astype(BF16)
    w_down_b = w_down.astype(BF16)
    gains = jnp.stack(
        [g_q_a, g_k_a, g_k_idx, g_q_b, g_k_b, g_q_m], axis=1)[..., None]

    inv_freq = jnp.power(
        ROPE_THETA, -jnp.arange(ROT_HALF, dtype=F32) / ROT_HALF)[:, None]
    pos3 = positions.reshape(B, 1, S)

    n_f = 2 * REL_CLIP
    fvec = jnp.concatenate(
        [rel_bias[..., 1:], jnp.broadcast_to(rel_bias[..., -1:],
                                             rel_bias.shape[:-1] + (n_f,))],
        axis=-1)[:, :, None, :]
    bias = _bias_call(fvec)

    for l in range(depth):
        qa, ka, va, qi, ki, wi, qb, kb, vb, qm = _proj_call(
            x, pos3, inv_freq, g_mix[l][None], w_in_t, l, gains[l])
        mk, mv = _memkv_call(mem, g_mem[l][None], w_mem_t, l, g_k_m[l][:, None])
        oa = _dsa_call(qa, ka, va, qi, ki, wi)
        obm = _band_mem_call(qb, kb, vb, bias, l, qm, mk, mv)
        x = _out_ffn_call(
            x.reshape(B * S, D), oa.reshape(B * S, WIDTH_A),
            obm.reshape(B * S, WIDTH_B + WIDTH_M), w_out_b, g_ffn[:, None],
            w_gu_b, w_down_b, l).reshape(B, S, D)
    return x
```

```python
import functools

import jax
import jax.numpy as jnp
from jax import lax
from jax.experimental import pallas as pl
from jax.experimental.pallas import tpu as pltpu

F32 = jnp.float32
BF16 = jnp.bfloat16
I32 = jnp.int32

D_MODEL = 1024
HEAD_DIM = 64
N_HEADS_A = 6
N_HEADS_B = 6
N_HEADS_M = 4
WIDTH_A = N_HEADS_A * HEAD_DIM
WIDTH_B = N_HEADS_B * HEAD_DIM
WIDTH_M = N_HEADS_M * HEAD_DIM
IDX_HEADS = 4
IDX_DIM = 64
ROT_DIM = HEAD_DIM // 4
ROT_HALF = ROT_DIM // 2
ROPE_THETA = 500000.0
CHUNK = 64
PREV_CHUNKS = 8
REL_CLIP = 256
TOPK_MAX = 256
D_FF = 2816
EPS = 1e-6
LOG2E = 1.4426950408889634

LANES = 128
SUBLANES = 8
PAIR = 2 * HEAD_DIM
V_ROWS = HEAD_DIM + 16
TM = 1024
TF = 512
TQ = 256
KB = 256
BAND_BLOCKS = 3
WI_ROWS = 8
INT_MIN = -(2 ** 31)
KEY_BITS = 32
NEG_INF = float("-inf")
MASKED = -1e30
VMEM_LIMIT = 56 * 1024 * 1024

R_QA = 0
R_KA = R_QA + WIDTH_A
R_VA = R_KA + WIDTH_A
R_QI = R_VA + WIDTH_A
R_KI = R_QI + IDX_HEADS * IDX_DIM
R_WI = R_KI + IDX_DIM
R_QB = R_WI + IDX_HEADS
R_KB = R_QB + WIDTH_B
R_VB = R_KB + WIDTH_B
R_QM = R_VB + WIDTH_B
R_END = R_QM + WIDTH_M

NT_DIMS = (((1,), (1,)), ((), ()))


def _dot(a, b):
    return jnp.dot(a, b, preferred_element_type=F32)


def _dot_nt(a, b):
    return lax.dot_general(a, b, NT_DIMS, preferred_element_type=F32)


def _chunk_of(pos):
    return jnp.right_shift(pos, CHUNK.bit_length() - 1)


def _tree_sum(xs):
    while len(xs) > 1:
        xs = [a + b for a, b in zip(xs[0::2], xs[1::2])] + xs[len(xs) & ~1:]
    return xs[0]


def _as_i32(v):
    return v - (1 << 32) if v >= (1 << 31) else v


def _bit_transpose32(rows):
    rows = list(rows)
    j, m = 16, 0x0000FFFF
    while j:
        k = 0
        while k < 32:
            t = ((rows[k] ^ lax.shift_right_logical(rows[k + j], jnp.int32(j)))
                 & jnp.int32(_as_i32(m)))
            rows[k] = rows[k] ^ t
            rows[k + j] = rows[k + j] ^ lax.shift_left(t, jnp.int32(j))
            k = (k + j + 1) & ~j
        j >>= 1
        if j:
            m = (m ^ (m << j)) & 0xFFFFFFFF
    return rows


def _row_rms(x, g):
    ms = jnp.mean(x * x, axis=-1, keepdims=True)
    return x * lax.rsqrt(ms + EPS) * g


def _head_norm_t(blk, g_col):
    ms = jnp.mean(blk * blk, axis=0, keepdims=True)
    return blk * lax.rsqrt(ms + EPS) * g_col


def _rope_t(y, cos, sin):
    y1 = y[0:ROT_HALF]
    y2 = y[ROT_HALF:ROT_DIM]
    return jnp.concatenate(
        [y1 * cos - y2 * sin, y2 * cos + y1 * sin, y[ROT_DIM:]], axis=0)


def _with_ones(v_heads, n_heads):
    ones = jnp.ones((V_ROWS - HEAD_DIM, v_heads.shape[1]), v_heads.dtype)
    parts = []
    for i in range(n_heads):
        parts += [v_heads[i * HEAD_DIM:(i + 1) * HEAD_DIM], ones]
    return jnp.concatenate(parts, axis=0)


def _pair_slot(y, odd):
    z = jnp.zeros_like(y)
    return jnp.concatenate([z, y] if odd else [y, z], axis=0)


def _proj_kernel(x_ref, pos_ref, invf_ref, gmix_ref, wt_ref, gains_ref,
                 qa_ref, ka_ref, va_ref, qi_ref, ki_ref, wi_ref,
                 qb_ref, kb_ref, vb_ref, qm_ref):
    h = _row_rms(x_ref[0], gmix_ref[...]).astype(BF16)
    ang = invf_ref[...] * pos_ref[0].astype(F32)
    cos = jnp.cos(ang)
    sin = jnp.sin(ang)
    q_scale = HEAD_DIM ** -0.5 * LOG2E
    n_blk = TM // KB

    def proj_t(r0, r1):
        return _dot_nt(wt_ref[r0:r1, :], h)

    def gain(i):
        return gains_ref[i]

    def head(t, i):
        return t[i * HEAD_DIM:(i + 1) * HEAD_DIM]

    def store_q(ref, t, n_heads, g, rope):
        for i in range(n_heads):
            y = head(t, i)
            if g is not None:
                y = _head_norm_t(y, g)
            if rope:
                y = _rope_t(y, cos, sin)
            ref[0, i * PAIR:(i + 1) * PAIR, :] = _pair_slot(
                y * q_scale, i % 2 == 1).astype(BF16)

    def store_k(ref, t, n_heads, g, rope):
        for p in range(n_heads // 2):
            ys = []
            for i in (2 * p, 2 * p + 1):
                y = _head_norm_t(head(t, i), g)
                ys.append(_rope_t(y, cos, sin) if rope else y)
            blk = jnp.concatenate(ys, axis=0).T.astype(BF16)
            for j in range(n_blk):
                ref[0, j, :, p * PAIR:(p + 1) * PAIR] = blk[j * KB:(j + 1) * KB]

    def store_v(ref, t, n_heads):
        tb = _with_ones(t.astype(BF16), n_heads)
        for j in range(n_blk):
            ref[0, j] = tb[:, j * KB:(j + 1) * KB]

    store_q(qa_ref, proj_t(R_QA, R_KA), N_HEADS_A, gain(0), True)
    store_k(ka_ref, proj_t(R_KA, R_VA), N_HEADS_A, gain(1), True)
    store_v(va_ref, proj_t(R_VA, R_QI), N_HEADS_A)

    t = proj_t(R_QI, R_WI + WI_ROWS)
    for i in range(IDX_HEADS):
        y = _rope_t(head(t, i), cos, sin) * (IDX_DIM ** -0.5)
        qi_ref[0, i * PAIR:(i + 1) * PAIR, :] = _pair_slot(y, False).astype(BF16)
    ki = _rope_t(_head_norm_t(head(t, IDX_HEADS), gain(2)), cos, sin)
    ki_blk = _pair_slot(ki, False).T.astype(BF16)
    for j in range(n_blk):
        ki_ref[0, j] = ki_blk[j * KB:(j + 1) * KB]
    wi_ref[0] = t[R_WI - R_QI:R_WI - R_QI + WI_ROWS] * (IDX_HEADS ** -0.5)

    store_q(qb_ref, proj_t(R_QB, R_KB), N_HEADS_B, gain(3), False)
    store_k(kb_ref, proj_t(R_KB, R_VB), N_HEADS_B, gain(4), False)
    store_v(vb_ref, proj_t(R_VB, R_QM), N_HEADS_B)
    store_q(qm_ref, proj_t(R_QM, R_END), N_HEADS_M, gain(5), False)


def _proj_call(x, pos3, invf, gmix, wt, layer, gains):
    B, S, _ = x.shape
    nkb = S // KB
    n_blk = TM // KB

    def tok_t(rows):
        return pl.BlockSpec((1, rows, TM), lambda b, i: (b, 0, i))

    def full(shape):
        return pl.BlockSpec(shape, lambda b, i: (0,) * len(shape))

    def krow(width):
        return pl.BlockSpec((1, n_blk, KB, width), lambda b, i: (b, i, 0, 0))

    def vt(rows):
        return pl.BlockSpec((1, n_blk, rows, KB), lambda b, i: (b, i, 0, 0))

    out_shape = (
        jax.ShapeDtypeStruct((B, N_HEADS_A * PAIR, S), BF16),
        jax.ShapeDtypeStruct((B, nkb, KB, WIDTH_A), BF16),
        jax.ShapeDtypeStruct((B, nkb, N_HEADS_A * V_ROWS, KB), BF16),
        jax.ShapeDtypeStruct((B, IDX_HEADS * PAIR, S), BF16),
        jax.ShapeDtypeStruct((B, nkb, KB, PAIR), BF16),
        jax.ShapeDtypeStruct((B, WI_ROWS, S), F32),
        jax.ShapeDtypeStruct((B, N_HEADS_B * PAIR, S), BF16),
        jax.ShapeDtypeStruct((B, nkb, KB, WIDTH_B), BF16),
        jax.ShapeDtypeStruct((B, nkb, N_HEADS_B * V_ROWS, KB), BF16),
        jax.ShapeDtypeStruct((B, N_HEADS_M * PAIR, S), BF16),
    )
    out_specs = (
        tok_t(N_HEADS_A * PAIR), krow(WIDTH_A), vt(N_HEADS_A * V_ROWS),
        tok_t(IDX_HEADS * PAIR), krow(PAIR), tok_t(WI_ROWS),
        tok_t(N_HEADS_B * PAIR), krow(WIDTH_B), vt(N_HEADS_B * V_ROWS),
        tok_t(N_HEADS_M * PAIR),
    )
    return pl.pallas_call(
        _proj_kernel,
        grid=(B, S // TM),
        in_specs=[
            pl.BlockSpec((1, TM, D_MODEL), lambda b, i: (b, i, 0)),
            pl.BlockSpec((1, 1, TM), lambda b, i: (b, 0, i)),
            full((ROT_HALF, 1)),
            full((1, D_MODEL)),
            pl.BlockSpec((None, R_END, D_MODEL), lambda b, i: (layer, 0, 0)),
            full((6, HEAD_DIM, 1)),
        ],
        out_specs=out_specs,
        out_shape=out_shape,
        compiler_params=pltpu.CompilerParams(
            dimension_semantics=("parallel", "parallel"),
            vmem_limit_bytes=VMEM_LIMIT),
        name="in_proj",
    )(x, pos3, invf, gmix, wt, gains)


def _memkv_kernel(mem_ref, gmem_ref, wt_ref, gk_ref, mk_ref, mv_ref):
    hm = _row_rms(mem_ref[0], gmem_ref[...]).astype(BF16)
    kt = _dot_nt(wt_ref[0:WIDTH_M, :], hm)
    for p in range(N_HEADS_M // 2):
        ys = [_head_norm_t(kt[i * HEAD_DIM:(i + 1) * HEAD_DIM], gk_ref[...])
              for i in (2 * p, 2 * p + 1)]
        mk_ref[0, :, p * PAIR:(p + 1) * PAIR] = (
            jnp.concatenate(ys, axis=0).T.astype(BF16))
    mv_ref[0] = _with_ones(
        _dot_nt(wt_ref[WIDTH_M:2 * WIDTH_M, :], hm).astype(BF16), N_HEADS_M)


def _memkv_call(mem, gmem, wt, layer, gk):
    B, n_mem, _ = mem.shape
    return pl.pallas_call(
        _memkv_kernel,
        grid=(B,),
        in_specs=[
            pl.BlockSpec((1, n_mem, D_MODEL), lambda b: (b, 0, 0)),
            pl.BlockSpec((1, D_MODEL), lambda b: (0, 0)),
            pl.BlockSpec((None, 2 * WIDTH_M, D_MODEL), lambda b: (layer, 0, 0)),
            pl.BlockSpec((HEAD_DIM, 1), lambda b: (0, 0)),
        ],
        out_specs=(
            pl.BlockSpec((1, n_mem, WIDTH_M), lambda b: (b, 0, 0)),
            pl.BlockSpec((1, N_HEADS_M * V_ROWS, n_mem), lambda b: (b, 0, 0)),
        ),
        out_shape=(
            jax.ShapeDtypeStruct((B, n_mem, WIDTH_M), BF16),
            jax.ShapeDtypeStruct((B, N_HEADS_M * V_ROWS, n_mem), BF16),
        ),
        compiler_params=pltpu.CompilerParams(
            dimension_semantics=("parallel",), vmem_limit_bytes=VMEM_LIMIT),
        name="mem_kv",
    )(mem, gmem, wt, gk)


def _bias_kernel(f_ref, out_ref):
    n_keys = BAND_BLOCKS * KB
    width = f_ref.shape[-1]
    base = jnp.broadcast_to(f_ref[0, 0], (n_keys, width))
    rolled = pltpu.roll(base, width - (n_keys - 1), 1, stride=1, stride_axis=0)
    tab = rolled[:, 0:TQ]
    key_c = _chunk_of(lax.broadcasted_iota(I32, (n_keys, TQ), 0))
    qry_c = _chunk_of(lax.broadcasted_iota(I32, (n_keys, TQ), 1)
                      + (BAND_BLOCKS - 1) * KB)
    ok = (key_c <= qry_c) & (key_c >= qry_c - PREV_CHUNKS)
    out_ref[0, 0] = jnp.where(ok, tab * LOG2E, NEG_INF)


def _bias_call(fvec):
    depth, n_heads, _, width = fvec.shape
    n_keys = BAND_BLOCKS * KB
    return pl.pallas_call(
        _bias_kernel,
        grid=(depth, n_heads),
        in_specs=[pl.BlockSpec((1, 1, 1, width), lambda l, h: (l, h, 0, 0))],
        out_specs=pl.BlockSpec((1, 1, n_keys, TQ), lambda l, h: (l, h, 0, 0)),
        out_shape=jax.ShapeDtypeStruct((depth, n_heads, n_keys, TQ), F32),
        compiler_params=pltpu.CompilerParams(
            dimension_semantics=("parallel", "parallel")),
        name="band_bias",
    )(fvec)


def _col_max(x):
    return jnp.max(x, axis=0, keepdims=True)


def _col_sum(x):
    return jnp.sum(x, axis=0, keepdims=True)


def _finish_pair(outs):
    return jnp.concatenate(outs, axis=0).T.astype(BF16)


def _dsa_kernel(qa_ref, ka_ref, va_ref, qi_ref, ki_ref, wi_ref, out_ref,
                keys_ref, planes_ref, mask_ref, acc_ref):
    t = pl.program_id(1)
    nkb = t + 1
    n_kb = keys_ref.shape[0]
    n_pair = nkb // 2
    odd = nkb % 2 == 1
    qry_c = _chunk_of(t * TQ + lax.broadcasted_iota(I32, (KB, TQ), 1))
    key_c0 = _chunk_of(lax.broadcasted_iota(I32, (KB, TQ), 0))

    def score_blocks(kbs):
        dss = [[_dot(ki_ref[0, kb], qi_ref[0, i * PAIR:(i + 1) * PAIR, :])
                for i in range(IDX_HEADS)] for kb in kbs]
        for kb, ds in zip(kbs, dss):
            s = jnp.zeros((KB, TQ), F32)
            for i in range(IDX_HEADS):
                s = s + jnp.maximum(ds[i], 0.0) * wi_ref[0, i:i + 1, :]
            s = jnp.where(s == 0.0, 0.0, s)
            bits = lax.bitcast_convert_type(s, I32)
            key = jnp.where(bits >= 0, bits, bits ^ jnp.int32(0x7FFFFFFF))
            allowed = (key_c0 + kb * (KB // CHUNK)) <= qry_c
            key = jnp.where(allowed, key, jnp.int32(INT_MIN))
            keys_ref[kb] = key
            u = key ^ jnp.int32(INT_MIN)
            cols = _bit_transpose32(
                [u[j * SUBLANES:(j + 1) * SUBLANES] for j in range(KEY_BITS)])
            for b in range(KEY_BITS):
                planes_ref[b, kb] = cols[KEY_BITS - 1 - b]

    def score_pair(j, carry):
        score_blocks([2 * j, 2 * j + 1])
        return carry

    lax.fori_loop(0, n_pair, score_pair, 0)

    @pl.when(odd)
    def _():
        score_blocks([nkb - 1])

    @pl.when(t == 0)
    def _():
        planes_ref[:, 1:] = jnp.zeros_like(planes_ref[:, 1:])

    def col_count(words):
        return _col_sum(_tree_sum([lax.population_count(w) for w in words]))

    def bit(i, carry):
        active, rank, thr_u = carry
        b = KEY_BITS - 1 - i
        planes = planes_ref[b]
        ones = [active[kb] & planes[kb] for kb in range(n_kb)]
        c1 = col_count(ones)
        take = c1 >= rank
        active = tuple(jnp.where(take, ones[kb], active[kb] ^ ones[kb])
                       for kb in range(n_kb))
        rank = jnp.where(take, rank, rank - c1)
        thr_u = jnp.where(take, thr_u | lax.shift_left(jnp.int32(1), b), thr_u)
        return active, rank, thr_u

    active0 = tuple(jnp.full((SUBLANES, TQ), jnp.where(kb < nkb, -1, 0), I32)
                    for kb in range(n_kb))
    active, rank, thr_u = lax.fori_loop(
        0, KEY_BITS, bit,
        (active0, jnp.full((1, TQ), TOPK_MAX, I32), jnp.zeros((1, TQ), I32)))
    thr = thr_u ^ jnp.int32(INT_MIN)
    n_ge = TOPK_MAX - rank + col_count(active)
    tie = (n_ge > TOPK_MAX) & (thr > INT_MIN)
    any_tie = jnp.max(jnp.where(tie, 1.0, 0.0)) > 0.0

    @pl.when(jnp.logical_not(any_tie))
    def _():
        floor = jnp.maximum(thr, jnp.int32(INT_MIN + 1))

        def blk(kb, carry):
            mask_ref[kb] = jnp.where(keys_ref[kb] >= floor, 0.0, MASKED)
            return carry

        lax.fori_loop(0, nkb, blk, 0)

    @pl.when(any_tie)
    def _():
        lower = (lax.broadcasted_iota(I32, (KB, KB), 1)
                 < lax.broadcasted_iota(I32, (KB, KB), 0))
        ltri = jnp.where(lower, 1.0, 0.0).astype(BF16)
        need = rank.astype(F32)

        def blk(kb, seen):
            k = keys_ref[kb]
            eq = jnp.where(k == thr, 1.0, 0.0)
            before = _dot(ltri, eq.astype(BF16)) + seen
            keep = (k > thr) | ((k == thr) & (before < need))
            keep = keep & (k > INT_MIN)
            mask_ref[kb] = jnp.where(keep, 0.0, MASKED)
            return seen + _col_sum(eq)

        lax.fori_loop(0, nkb, blk, jnp.zeros((1, TQ), F32))

    acc_ref[...] = jnp.zeros_like(acc_ref)
    heads = range(N_HEADS_A)

    def attn_blocks(kbs, ms):
        lgs = []
        for kb in kbs:
            mask = mask_ref[kb]
            lgs.append([
                _dot(ka_ref[0, kb, :, (h // 2) * PAIR:(h // 2 + 1) * PAIR],
                     qa_ref[0, h * PAIR:(h + 1) * PAIR, :]) + mask
                for h in heads])
        ms_new = [functools.reduce(
            jnp.maximum, [ms[h]] + [_col_max(lg[h]) for lg in lgs])
            for h in heads]
        alphas = [jnp.exp2(ms[h] - ms_new[h]) for h in heads]
        es = [[jnp.exp2(lg[h] - ms_new[h]) for h in heads] for lg in lgs]
        pvs = [functools.reduce(jnp.add, [
            _dot(va_ref[0, kb, h * V_ROWS:(h + 1) * V_ROWS, :],
                 e[h].astype(BF16)) for kb, e in zip(kbs, es)])
            for h in heads]
        for h in heads:
            acc_ref[h] = acc_ref[h] * alphas[h] + pvs[h]
        return tuple(ms_new)

    def attn_pair(j, ms):
        return attn_blocks([2 * j, 2 * j + 1], ms)

    m0 = tuple(jnp.full((1, TQ), MASKED, F32) for _ in heads)
    ms = lax.fori_loop(0, n_pair, attn_pair, m0)
    lax.cond(odd, lambda: attn_blocks([nkb - 1], ms), lambda: ms)

    def head_out(h):
        return acc_ref[h, 0:HEAD_DIM] / acc_ref[h, HEAD_DIM:HEAD_DIM + 1]

    for p in range(N_HEADS_A // 2):
        outs = [head_out(h) for h in (2 * p, 2 * p + 1)]
        out_ref[0, :, p * PAIR:(p + 1) * PAIR] = _finish_pair(outs)


def _dsa_call(qa, ka, va, qi, ki, wi):
    B, nkb = ka.shape[0], ka.shape[1]
    S = nkb * KB

    def tile_t(rows):
        return pl.BlockSpec((1, rows, TQ), lambda b, t: (b, 0, t))

    def whole(a):
        return pl.BlockSpec((1,) + a.shape[1:], lambda b, t: (b, 0, 0, 0))

    return pl.pallas_call(
        _dsa_kernel,
        grid=(B, S // TQ),
        in_specs=[tile_t(N_HEADS_A * PAIR), whole(ka), whole(va),
                  tile_t(IDX_HEADS * PAIR), whole(ki), tile_t(WI_ROWS)],
        out_specs=pl.BlockSpec((1, TQ, WIDTH_A), lambda b, t: (b, t, 0)),
        out_shape=jax.ShapeDtypeStruct((B, S, WIDTH_A), BF16),
        scratch_shapes=[
            pltpu.VMEM((nkb, KB, TQ), I32),
            pltpu.VMEM((KEY_BITS, nkb, SUBLANES, TQ), I32),
            pltpu.VMEM((nkb, KB, TQ), F32),
            pltpu.VMEM((N_HEADS_A, V_ROWS, TQ), F32),
        ],
        compiler_params=pltpu.CompilerParams(
            dimension_semantics=("parallel", "arbitrary"),
            vmem_limit_bytes=VMEM_LIMIT),
        name="dsa_attn",
    )(qa, ka, va, qi, ki, wi)


def _band_mem_kernel(qb_ref, kb_ref, vb_ref, bias_ref, qm_ref, mk_ref, mv_ref,
                     out_ref):
    t = pl.program_id(1)
    srcs, pens = [], []
    for r in range(BAND_BLOCKS):
        blk = t - (BAND_BLOCKS - 1) + r
        srcs.append(jnp.maximum(blk, 0))
        pens.append(jnp.where(blk >= 0, 0.0, NEG_INF))

    def v_rows(h):
        return slice(h * V_ROWS, (h + 1) * V_ROWS)

    def normalised(pv):
        return pv[0:HEAD_DIM] / pv[HEAD_DIM:HEAD_DIM + 1]

    def pair_cols(h):
        return slice((h // 2) * PAIR, (h // 2 + 1) * PAIR)

    band_lg = [[_dot(kb_ref[0, srcs[r], :, pair_cols(h)],
                     qb_ref[0, h * PAIR:(h + 1) * PAIR, :])
                + bias_ref[0, h, r * KB:(r + 1) * KB, :] + pens[r]
                for r in range(BAND_BLOCKS)] for h in range(N_HEADS_B)]
    mem_lg = [_dot(mk_ref[0, :, pair_cols(h)],
                   qm_ref[0, h * PAIR:(h + 1) * PAIR, :])
              for h in range(N_HEADS_M)]

    band_e = []
    for lgs in band_lg:
        m = functools.reduce(jnp.maximum, [_col_max(lg) for lg in lgs])
        band_e.append([jnp.exp2(lg - m).astype(BF16) for lg in lgs])
    mem_e = [jnp.exp2(lg - _col_max(lg)).astype(BF16) for lg in mem_lg]

    band_o = [normalised(functools.reduce(jnp.add, [
        _dot(vb_ref[0, srcs[r], v_rows(h), :], band_e[h][r])
        for r in range(BAND_BLOCKS)])) for h in range(N_HEADS_B)]
    mem_o = [normalised(_dot(mv_ref[0, v_rows(h), :], mem_e[h]))
             for h in range(N_HEADS_M)]

    for p in range(N_HEADS_B // 2):
        out_ref[0, :, p * PAIR:(p + 1) * PAIR] = _finish_pair(
            band_o[2 * p:2 * p + 2])
    for p in range(N_HEADS_M // 2):
        out_ref[0, :, WIDTH_B + p * PAIR:WIDTH_B + (p + 1) * PAIR] = (
            _finish_pair(mem_o[2 * p:2 * p + 2]))


def _band_mem_call(qb, kb, vb, bias, layer, qm, mk, mv):
    B, nkb = kb.shape[0], kb.shape[1]
    S = nkb * KB
    n_mem = mk.shape[1]

    def tile_t(rows):
        return pl.BlockSpec((1, rows, TQ), lambda b, t: (b, 0, t))

    def whole(a):
        return pl.BlockSpec((1,) + a.shape[1:], lambda b, t: (b, 0, 0, 0))

    return pl.pallas_call(
        _band_mem_kernel,
        grid=(B, S // TQ),
        in_specs=[
            tile_t(N_HEADS_B * PAIR), whole(kb), whole(vb),
            pl.BlockSpec((1,) + bias.shape[1:], lambda b, t: (layer, 0, 0, 0)),
            tile_t(N_HEADS_M * PAIR),
            pl.BlockSpec((1, n_mem, WIDTH_M), lambda b, t: (b, 0, 0)),
            pl.BlockSpec((1, N_HEADS_M * V_ROWS, n_mem), lambda b, t: (b, 0, 0)),
        ],
        out_specs=pl.BlockSpec((1, TQ, WIDTH_B + WIDTH_M), lambda b, t: (b, t, 0)),
        out_shape=jax.ShapeDtypeStruct((B, S, WIDTH_B + WIDTH_M), BF16),
        compiler_params=pltpu.CompilerParams(
            dimension_semantics=("parallel", "parallel"),
            vmem_limit_bytes=VMEM_LIMIT),
        name="band_mem_attn",
    )(qb, kb, vb, bias, qm, mk, mv)


FF_CHUNK = 256


def _out_ffn_kernel(x_ref, oa_ref, obm_ref, wo_ref, gffn_ref, wgu_ref, wd_ref,
                    out_ref):
    attn = jnp.concatenate([oa_ref[...], obm_ref[...]], axis=1)
    x1 = x_ref[...] + _dot(attn, wo_ref[...])
    h = _row_rms(x1, gffn_ref[...]).astype(BF16)
    acc = x1
    for c in range(D_FF // FF_CHUNK):
        c0 = c * FF_CHUNK
        gate = _dot(h, wgu_ref[:, c0:c0 + FF_CHUNK])
        up = _dot(h, wgu_ref[:, D_FF + c0:D_FF + c0 + FF_CHUNK])
        act = gate * (1.0 / (1.0 + jnp.exp(-gate))) * up
        acc = acc + _dot(act.astype(BF16), wd_ref[c0:c0 + FF_CHUNK, :])
    out_ref[...] = acc


def _out_ffn_call(x2, oa2, obm2, wo, gffn, wgu, wd, layer):
    n = x2.shape[0]

    def rows(width):
        return pl.BlockSpec((TF, width), lambda i: (i, 0))

    def resident(a):
        return pl.BlockSpec((None,) + a.shape[1:], lambda i: (layer, 0, 0),
                            pipeline_mode=pl.Buffered(1))

    return pl.pallas_call(
        _out_ffn_kernel,
        grid=(n // TF,),
        in_specs=[rows(D_MODEL), rows(WIDTH_A), rows(WIDTH_B + WIDTH_M),
                  resident(wo), resident(gffn), resident(wgu), resident(wd)],
        out_specs=rows(D_MODEL),
        out_shape=jax.ShapeDtypeStruct((n, D_MODEL), F32),
        compiler_params=pltpu.CompilerParams(
            dimension_semantics=("parallel",), vmem_limit_bytes=VMEM_LIMIT),
        name="out_ffn",
    )(x2, oa2, obm2, wo, gffn, wgu, wd)


def kernel(x, mem, positions, g_mix, w_in, g_q_a, g_k_a, g_k_idx, g_q_b, g_k_b,
           rel_bias, g_q_m, g_k_m, g_mem, w_mem_kv, w_out, g_ffn, w_gate_up,
           w_down):
    B, S, D = x.shape
    depth = w_in.shape[0]

    w_in_t = jnp.swapaxes(w_in.astype(BF16), 1, 2)
    w_mem_t = jnp.swapaxes(w_mem_kv.astype(BF16), 1, 2)
    w_out_b = w_out.astype(BF16)
    w_gu_b = w_gate_up.astype(BF16)
    w_down_b = w_down.astype(BF16)
    gains = jnp.stack(
        [g_q_a, g_k_a, g_k_idx, g_q_b, g_k_b, g_q_m], axis=1)[..., None]

    inv_freq = jnp.power(
        ROPE_THETA, -jnp.arange(ROT_HALF, dtype=F32) / ROT_HALF)[:, None]
    pos3 = positions.reshape(B, 1, S)

    n_f = 2 * REL_CLIP
    fvec = jnp.concatenate(
        [rel_bias[..., 1:], jnp.broadcast_to(rel_bias[..., -1:],
                                             rel_bias.shape[:-1] + (n_f,))],
        axis=-1)[:, :, None, :]
    bias = _bias_call(fvec)

    for l in range(depth):
        qa, ka, va, qi, ki, wi, qb, kb, vb, qm = _proj_call(
            x, pos3, inv_freq, g_mix[l][None], w_in_t, l, gains[l])
        mk, mv = _memkv_call(mem, g_mem[l][None], w_mem_t, l, g_k_m[l][:, None])
        oa = _dsa_call(qa, ka, va, qi, ki, wi)
        obm = _band_mem_call(qb, kb, vb, bias, l, qm, mk, mv)
        x = _out_ffn_call(
            x.reshape(B * S, D), oa.reshape(B * S, WIDTH_A),
            obm.reshape(B * S, WIDTH_B + WIDTH_M), w_out_b, g_ffn[:, None],
            w_gu_b, w_down_b, l).reshape(B, S, D)
    return x
```

```python
import functools

import jax
import jax.numpy as jnp
from jax import lax
from jax.experimental import pallas as pl
from jax.experimental.pallas import tpu as pltpu

F32 = jnp.float32
BF16 = jnp.bfloat16
I32 = jnp.int32

D_MODEL = 1024
HEAD_DIM = 64
N_HEADS_A = 6
N_HEADS_B = 6
N_HEADS_M = 4
WIDTH_A = N_HEADS_A * HEAD_DIM
WIDTH_B = N_HEADS_B * HEAD_DIM
WIDTH_M = N_HEADS_M * HEAD_DIM
IDX_HEADS = 4
IDX_DIM = 64
ROT_DIM = HEAD_DIM // 4
ROT_HALF = ROT_DIM // 2
ROPE_THETA = 500000.0
CHUNK = 64
PREV_CHUNKS = 8
REL_CLIP = 256
TOPK_MAX = 256
D_FF = 2816
EPS = 1e-6
LOG2E = 1.4426950408889634

LANES = 128
SUBLANES = 8
PAIR = 2 * HEAD_DIM
V_ROWS = HEAD_DIM + 16
TM = 1024
TF = 512
TQ = 256
KB = 256
BAND_BLOCKS = 3
WI_ROWS = 8
INT_MIN = -(2 ** 31)
KEY_BITS = 32
NEG_INF = float("-inf")
MASKED = -1e30
SHIFT_LIMIT = 40.0
SHIFT_MARGIN = 1.02
VMEM_LIMIT = 56 * 1024 * 1024

R_QA = 0
R_KA = R_QA + WIDTH_A
R_VA = R_KA + WIDTH_A
R_QI = R_VA + WIDTH_A
R_KI = R_QI + IDX_HEADS * IDX_DIM
R_WI = R_KI + IDX_DIM
R_QB = R_WI + IDX_HEADS
R_KB = R_QB + WIDTH_B
R_VB = R_KB + WIDTH_B
R_QM = R_VB + WIDTH_B
R_END = R_QM + WIDTH_M

NT_DIMS = (((1,), (1,)), ((), ()))


def _dot(a, b):
    return jnp.dot(a, b, preferred_element_type=F32)


def _dot_nt(a, b):
    return lax.dot_general(a, b, NT_DIMS, preferred_element_type=F32)


def _chunk_of(pos):
    return jnp.right_shift(pos, CHUNK.bit_length() - 1)


def _tree_sum(xs):
    while len(xs) > 1:
        xs = [a + b for a, b in zip(xs[0::2], xs[1::2])] + xs[len(xs) & ~1:]
    return xs[0]


def _as_i32(v):
    return v - (1 << 32) if v >= (1 << 31) else v


def _bit_transpose32(rows):
    rows = list(rows)
    j, m = 16, 0x0000FFFF
    while j:
        k = 0
        while k < 32:
            t = ((rows[k] ^ lax.shift_right_logical(rows[k + j], jnp.int32(j)))
                 & jnp.int32(_as_i32(m)))
            rows[k] = rows[k] ^ t
            rows[k + j] = rows[k + j] ^ lax.shift_left(t, jnp.int32(j))
            k = (k + j + 1) & ~j
        j >>= 1
        if j:
            m = (m ^ (m << j)) & 0xFFFFFFFF
    return rows


def _row_rms(x, g):
    ms = jnp.mean(x * x, axis=-1, keepdims=True)
    return x * lax.rsqrt(ms + EPS) * g


def _head_norm_t(blk, g_col):
    ms = jnp.mean(blk * blk, axis=0, keepdims=True)
    return blk * lax.rsqrt(ms + EPS) * g_col


def _rope_t(y, cos, sin):
    y1 = y[0:ROT_HALF]
    y2 = y[ROT_HALF:ROT_DIM]
    return jnp.concatenate(
        [y1 * cos - y2 * sin, y2 * cos + y1 * sin, y[ROT_DIM:]], axis=0)


def _with_ones(v_heads, n_heads):
    ones = jnp.ones((V_ROWS - HEAD_DIM, v_heads.shape[1]), v_heads.dtype)
    parts = []
    for i in range(n_heads):
        parts += [v_heads[i * HEAD_DIM:(i + 1) * HEAD_DIM], ones]
    return jnp.concatenate(parts, axis=0)


def _pair_slot(y, odd):
    z = jnp.zeros_like(y)
    return jnp.concatenate([z, y] if odd else [y, z], axis=0)


def _proj_kernel(x_ref, pos_ref, invf_ref, gmix_ref, wt_ref, gains_ref,
                 qa_ref, ka_ref, va_ref, qi_ref, ki_ref, wi_ref,
                 qb_ref, kb_ref, vb_ref, qm_ref):
    h = _row_rms(x_ref[0], gmix_ref[...]).astype(BF16)
    ang = invf_ref[...] * pos_ref[0].astype(F32)
    cos = jnp.cos(ang)
    sin = jnp.sin(ang)
    q_scale = HEAD_DIM ** -0.5 * LOG2E
    n_blk = TM // KB

    def proj_t(r0, r1):
        return _dot_nt(wt_ref[r0:r1, :], h)

    def gain(i):
        return gains_ref[i]

    def head(t, i):
        return t[i * HEAD_DIM:(i + 1) * HEAD_DIM]

    def store_q(ref, t, n_heads, g, rope):
        for i in range(n_heads):
            y = head(t, i)
            if g is not None:
                y = _head_norm_t(y, g)
            if rope:
                y = _rope_t(y, cos, sin)
            ref[0, i * PAIR:(i + 1) * PAIR, :] = _pair_slot(
                y * q_scale, i % 2 == 1).astype(BF16)

    def store_k(ref, t, n_heads, g, rope):
        for p in range(n_heads // 2):
            ys = []
            for i in (2 * p, 2 * p + 1):
                y = _head_norm_t(head(t, i), g)
                ys.append(_rope_t(y, cos, sin) if rope else y)
            blk = jnp.concatenate(ys, axis=0).T.astype(BF16)
            for j in range(n_blk):
                ref[0, j, :, p * PAIR:(p + 1) * PAIR] = blk[j * KB:(j + 1) * KB]

    def store_v(ref, t, n_heads):
        tb = _with_ones(t.astype(BF16), n_heads)
        for j in range(n_blk):
            ref[0, j] = tb[:, j * KB:(j + 1) * KB]

    store_q(qa_ref, proj_t(R_QA, R_KA), N_HEADS_A, gain(0), True)
    store_k(ka_ref, proj_t(R_KA, R_VA), N_HEADS_A, gain(1), True)
    store_v(va_ref, proj_t(R_VA, R_QI), N_HEADS_A)

    t = proj_t(R_QI, R_WI + WI_ROWS)
    for i in range(IDX_HEADS):
        y = _rope_t(head(t, i), cos, sin) * (IDX_DIM ** -0.5)
        qi_ref[0, i * PAIR:(i + 1) * PAIR, :] = _pair_slot(y, False).astype(BF16)
    ki = _rope_t(_head_norm_t(head(t, IDX_HEADS), gain(2)), cos, sin)
    ki_blk = _pair_slot(ki, False).T.astype(BF16)
    for j in range(n_blk):
        ki_ref[0, j] = ki_blk[j * KB:(j + 1) * KB]
    wi_ref[0] = t[R_WI - R_QI:R_WI - R_QI + WI_ROWS] * (IDX_HEADS ** -0.5)

    store_q(qb_ref, proj_t(R_QB, R_KB), N_HEADS_B, gain(3), False)
    store_k(kb_ref, proj_t(R_KB, R_VB), N_HEADS_B, gain(4), False)
    store_v(vb_ref, proj_t(R_VB, R_QM), N_HEADS_B)
    store_q(qm_ref, proj_t(R_QM, R_END), N_HEADS_M, gain(5), False)


def _proj_call(x, pos3, invf, gmix, wt, layer, gains):
    B, S, _ = x.shape
    nkb = S // KB
    n_blk = TM // KB

    def tok_t(rows):
        return pl.BlockSpec((1, rows, TM), lambda b, i: (b, 0, i))

    def full(shape):
        return pl.BlockSpec(shape, lambda b, i: (0,) * len(shape))

    def krow(width):
        return pl.BlockSpec((1, n_blk, KB, width), lambda b, i: (b, i, 0, 0))

    def vt(rows):
        return pl.BlockSpec((1, n_blk, rows, KB), lambda b, i: (b, i, 0, 0))

    out_shape = (
        jax.ShapeDtypeStruct((B, N_HEADS_A * PAIR, S), BF16),
        jax.ShapeDtypeStruct((B, nkb, KB, WIDTH_A), BF16),
        jax.ShapeDtypeStruct((B, nkb, N_HEADS_A * V_ROWS, KB), BF16),
        jax.ShapeDtypeStruct((B, IDX_HEADS * PAIR, S), BF16),
        jax.ShapeDtypeStruct((B, nkb, KB, PAIR), BF16),
        jax.ShapeDtypeStruct((B, WI_ROWS, S), F32),
        jax.ShapeDtypeStruct((B, N_HEADS_B * PAIR, S), BF16),
        jax.ShapeDtypeStruct((B, nkb, KB, WIDTH_B), BF16),
        jax.ShapeDtypeStruct((B, nkb, N_HEADS_B * V_ROWS, KB), BF16),
        jax.ShapeDtypeStruct((B, N_HEADS_M * PAIR, S), BF16),
    )
    out_specs = (
        tok_t(N_HEADS_A * PAIR), krow(WIDTH_A), vt(N_HEADS_A * V_ROWS),
        tok_t(IDX_HEADS * PAIR), krow(PAIR), tok_t(WI_ROWS),
        tok_t(N_HEADS_B * PAIR), krow(WIDTH_B), vt(N_HEADS_B * V_ROWS),
        tok_t(N_HEADS_M * PAIR),
    )
    return pl.pallas_call(
        _proj_kernel,
        grid=(B, S // TM),
        in_specs=[
            pl.BlockSpec((1, TM, D_MODEL), lambda b, i: (b, i, 0)),
            pl.BlockSpec((1, 1, TM), lambda b, i: (b, 0, i)),
            full((ROT_HALF, 1)),
            full((1, D_MODEL)),
            pl.BlockSpec((None, R_END, D_MODEL), lambda b, i: (layer, 0, 0)),
            full((6, HEAD_DIM, 1)),
        ],
        out_specs=out_specs,
        out_shape=out_shape,
        compiler_params=pltpu.CompilerParams(
            dimension_semantics=("parallel", "parallel"),
            vmem_limit_bytes=VMEM_LIMIT),
        name="in_proj",
    )(x, pos3, invf, gmix, wt, gains)


def _memkv_kernel(mem_ref, gmem_ref, wt_ref, gk_ref, mk_ref, mv_ref):
    hm = _row_rms(mem_ref[0], gmem_ref[...]).astype(BF16)
    kt = _dot_nt(wt_ref[0:WIDTH_M, :], hm)
    for p in range(N_HEADS_M // 2):
        ys = [_head_norm_t(kt[i * HEAD_DIM:(i + 1) * HEAD_DIM], gk_ref[...])
              for i in (2 * p, 2 * p + 1)]
        mk_ref[0, :, p * PAIR:(p + 1) * PAIR] = (
            jnp.concatenate(ys, axis=0).T.astype(BF16))
    mv_ref[0] = _with_ones(
        _dot_nt(wt_ref[WIDTH_M:2 * WIDTH_M, :], hm).astype(BF16), N_HEADS_M)


def _memkv_call(mem, gmem, wt, layer, gk):
    B, n_mem, _ = mem.shape
    return pl.pallas_call(
        _memkv_kernel,
        grid=(B,),
        in_specs=[
            pl.BlockSpec((1, n_mem, D_MODEL), lambda b: (b, 0, 0)),
            pl.BlockSpec((1, D_MODEL), lambda b: (0, 0)),
            pl.BlockSpec((None, 2 * WIDTH_M, D_MODEL), lambda b: (layer, 0, 0)),
            pl.BlockSpec((HEAD_DIM, 1), lambda b: (0, 0)),
        ],
        out_specs=(
            pl.BlockSpec((1, n_mem, WIDTH_M), lambda b: (b, 0, 0)),
            pl.BlockSpec((1, N_HEADS_M * V_ROWS, n_mem), lambda b: (b, 0, 0)),
        ),
        out_shape=(
            jax.ShapeDtypeStruct((B, n_mem, WIDTH_M), BF16),
            jax.ShapeDtypeStruct((B, N_HEADS_M * V_ROWS, n_mem), BF16),
        ),
        compiler_params=pltpu.CompilerParams(
            dimension_semantics=("parallel",), vmem_limit_bytes=VMEM_LIMIT),
        name="mem_kv",
    )(mem, gmem, wt, gk)


def _bias_kernel(bnd_ref, f_ref, out_ref):
    n_keys = BAND_BLOCKS * KB
    width = f_ref.shape[-1]
    base = jnp.broadcast_to(f_ref[0, 0], (n_keys, width))
    rolled = pltpu.roll(base, width - (n_keys - 1), 1, stride=1, stride_axis=0)
    tab = rolled[:, 0:TQ]
    key_c = _chunk_of(lax.broadcasted_iota(I32, (n_keys, TQ), 0))
    qry_c = _chunk_of(lax.broadcasted_iota(I32, (n_keys, TQ), 1)
                      + (BAND_BLOCKS - 1) * KB)
    ok = (key_c <= qry_c) & (key_c >= qry_c - PREV_CHUNKS)
    shift = bnd_ref[pl.program_id(0), 1]
    out_ref[0, 0] = jnp.where(ok, tab * LOG2E - shift, NEG_INF)


def _bias_call(bounds, fvec):
    depth, n_heads, _, width = fvec.shape
    n_keys = BAND_BLOCKS * KB
    return pl.pallas_call(
        _bias_kernel,
        grid=(depth, n_heads),
        in_specs=[pl.BlockSpec(memory_space=pltpu.SMEM),
                  pl.BlockSpec((1, 1, 1, width), lambda l, h: (l, h, 0, 0))],
        out_specs=pl.BlockSpec((1, 1, n_keys, TQ), lambda l, h: (l, h, 0, 0)),
        out_shape=jax.ShapeDtypeStruct((depth, n_heads, n_keys, TQ), F32),
        compiler_params=pltpu.CompilerParams(
            dimension_semantics=("parallel", "parallel")),
        name="band_bias",
    )(bounds, fvec)


def _col_max(x):
    return jnp.max(x, axis=0, keepdims=True)


def _col_sum(x):
    return jnp.sum(x, axis=0, keepdims=True)


def _finish_pair(outs):
    return jnp.concatenate(outs, axis=0).T.astype(BF16)


def _dsa_kernel(bnd_ref, qa_ref, ka_ref, va_ref, qi_ref, ki_ref, wi_ref, out_ref,
                keys_ref, planes_ref, mask_ref, acc_ref, *, layer):
    t = pl.program_id(1)
    shift = bnd_ref[layer, 0]
    nkb = t + 1
    n_kb = keys_ref.shape[0]
    n_pair = nkb // 2
    odd = nkb % 2 == 1
    qry_c = _chunk_of(t * TQ + lax.broadcasted_iota(I32, (KB, TQ), 1))
    key_c0 = _chunk_of(lax.broadcasted_iota(I32, (KB, TQ), 0))

    def score_blocks(kbs):
        dss = [[_dot(ki_ref[0, kb], qi_ref[0, i * PAIR:(i + 1) * PAIR, :])
                for i in range(IDX_HEADS)] for kb in kbs]
        for kb, ds in zip(kbs, dss):
            s = jnp.zeros((KB, TQ), F32)
            for i in range(IDX_HEADS):
                s = s + jnp.maximum(ds[i], 0.0) * wi_ref[0, i:i + 1, :]
            s = jnp.where(s == 0.0, 0.0, s)
            bits = lax.bitcast_convert_type(s, I32)
            key = jnp.where(bits >= 0, bits, bits ^ jnp.int32(0x7FFFFFFF))
            allowed = (key_c0 + kb * (KB // CHUNK)) <= qry_c
            key = jnp.where(allowed, key, jnp.int32(INT_MIN))
            keys_ref[kb] = key
            u = key ^ jnp.int32(INT_MIN)
            cols = _bit_transpose32(
                [u[j * SUBLANES:(j + 1) * SUBLANES] for j in range(KEY_BITS)])
            for b in range(KEY_BITS):
                planes_ref[b, kb] = cols[KEY_BITS - 1 - b]

    def score_pair(j, carry):
        score_blocks([2 * j, 2 * j + 1])
        return carry

    lax.fori_loop(0, n_pair, score_pair, 0)

    @pl.when(odd)
    def _():
        score_blocks([nkb - 1])

    @pl.when(t == 0)
    def _():
        planes_ref[:, 1:] = jnp.zeros_like(planes_ref[:, 1:])

    def col_count(words):
        return _col_sum(_tree_sum([lax.population_count(w) for w in words]))

    def bit(i, carry):
        active, rank, thr_u = carry
        b = KEY_BITS - 1 - i
        planes = planes_ref[b]
        ones = [active[kb] & planes[kb] for kb in range(n_kb)]
        c1 = col_count(ones)
        take = c1 >= rank
        active = tuple(jnp.where(take, ones[kb], active[kb] ^ ones[kb])
                       for kb in range(n_kb))
        rank = jnp.where(take, rank, rank - c1)
        thr_u = jnp.where(take, thr_u | lax.shift_left(jnp.int32(1), b), thr_u)
        return active, rank, thr_u

    active0 = tuple(jnp.full((SUBLANES, TQ), jnp.where(kb < nkb, -1, 0), I32)
                    for kb in range(n_kb))
    active, rank, thr_u = lax.fori_loop(
        0, KEY_BITS, bit,
        (active0, jnp.full((1, TQ), TOPK_MAX, I32), jnp.zeros((1, TQ), I32)))
    thr = thr_u ^ jnp.int32(INT_MIN)
    n_ge = TOPK_MAX - rank + col_count(active)
    tie = (n_ge > TOPK_MAX) & (thr > INT_MIN)
    any_tie = jnp.max(jnp.where(tie, 1.0, 0.0)) > 0.0

    @pl.when(jnp.logical_not(any_tie))
    def _():
        floor = jnp.maximum(thr, jnp.int32(INT_MIN + 1))

        def blk(kb, carry):
            mask_ref[kb] = jnp.where(keys_ref[kb] >= floor, -shift, MASKED)
            return carry

        lax.fori_loop(0, nkb, blk, 0)

    @pl.when(any_tie)
    def _():
        lower = (lax.broadcasted_iota(I32, (KB, KB), 1)
                 < lax.broadcasted_iota(I32, (KB, KB), 0))
        ltri = jnp.where(lower, 1.0, 0.0).astype(BF16)
        need = rank.astype(F32)

        def blk(kb, seen):
            k = keys_ref[kb]
            eq = jnp.where(k == thr, 1.0, 0.0)
            before = _dot(ltri, eq.astype(BF16)) + seen
            keep = (k > thr) | ((k == thr) & (before < need))
            keep = keep & (k > INT_MIN)
            mask_ref[kb] = jnp.where(keep, -shift, MASKED)
            return seen + _col_sum(eq)

        lax.fori_loop(0, nkb, blk, jnp.zeros((1, TQ), F32))

    acc_ref[...] = jnp.zeros_like(acc_ref)
    heads = range(N_HEADS_A)

    def logits(kbs):
        masks = [mask_ref[kb] for kb in kbs]
        return [[_dot(ka_ref[0, kb, :, (h // 2) * PAIR:(h // 2 + 1) * PAIR],
                      qa_ref[0, h * PAIR:(h + 1) * PAIR, :]) + mask
                 for kb, mask in zip(kbs, masks)] for h in heads]

    def weighted_values(h, kbs, es):
        return functools.reduce(jnp.add, [
            _dot(va_ref[0, kb, h * V_ROWS:(h + 1) * V_ROWS, :], e.astype(BF16))
            for kb, e in zip(kbs, es)])

    @pl.when(shift <= SHIFT_LIMIT)
    def _():
        def blocks(kbs):
            lgs = logits(kbs)
            for h in heads:
                acc_ref[h] += weighted_values(
                    h, kbs, [jnp.exp2(lg) for lg in lgs[h]])

        def pair(j, carry):
            blocks([2 * j, 2 * j + 1])
            return carry

        lax.fori_loop(0, n_pair, pair, 0)

        @pl.when(odd)
        def _():
            blocks([nkb - 1])

    @pl.when(shift > SHIFT_LIMIT)
    def _():
        def blocks(kbs, ms):
            lgs = logits(kbs)
            ms_new = []
            for h in heads:
                m_new = functools.reduce(
                    jnp.maximum, [ms[h]] + [_col_max(lg) for lg in lgs[h]])
                acc_ref[h] = (acc_ref[h] * jnp.exp2(ms[h] - m_new)
                              + weighted_values(
                                  h, kbs, [jnp.exp2(lg - m_new) for lg in lgs[h]]))
                ms_new.append(m_new)
            return tuple(ms_new)

        def pair(j, ms):
            return blocks([2 * j, 2 * j + 1], ms)

        m0 = tuple(jnp.full((1, TQ), MASKED, F32) for _ in heads)
        ms = lax.fori_loop(0, n_pair, pair, m0)

        @pl.when(odd)
        def _():
            blocks([nkb - 1], ms)

    def head_out(h):
        return acc_ref[h, 0:HEAD_DIM] / acc_ref[h, HEAD_DIM:HEAD_DIM + 1]

    for p in range(N_HEADS_A // 2):
        outs = [head_out(h) for h in (2 * p, 2 * p + 1)]
        out_ref[0, :, p * PAIR:(p + 1) * PAIR] = _finish_pair(outs)


def _dsa_call(bounds, layer, qa, ka, va, qi, ki, wi):
    B, nkb = ka.shape[0], ka.shape[1]
    S = nkb * KB

    def tile_t(rows):
        return pl.BlockSpec((1, rows, TQ), lambda b, t: (b, 0, t))

    def whole(a):
        return pl.BlockSpec((1,) + a.shape[1:], lambda b, t: (b, 0, 0, 0))

    return pl.pallas_call(
        functools.partial(_dsa_kernel, layer=layer),
        grid=(B, S // TQ),
        in_specs=[pl.BlockSpec(memory_space=pltpu.SMEM), tile_t(N_HEADS_A * PAIR), whole(ka), whole(va),
                  tile_t(IDX_HEADS * PAIR), whole(ki), tile_t(WI_ROWS)],
        out_specs=pl.BlockSpec((1, TQ, WIDTH_A), lambda b, t: (b, t, 0)),
        out_shape=jax.ShapeDtypeStruct((B, S, WIDTH_A), BF16),
        scratch_shapes=[
            pltpu.VMEM((nkb, KB, TQ), I32),
            pltpu.VMEM((KEY_BITS, nkb, SUBLANES, TQ), I32),
            pltpu.VMEM((nkb, KB, TQ), F32),
            pltpu.VMEM((N_HEADS_A, V_ROWS, TQ), F32),
        ],
        compiler_params=pltpu.CompilerParams(
            dimension_semantics=("parallel", "arbitrary"),
            vmem_limit_bytes=VMEM_LIMIT),
        name="dsa_attn",
    )(bounds, qa, ka, va, qi, ki, wi)


def _band_mem_kernel(bnd_ref, qb_ref, kb_ref, vb_ref, bias_ref, qm_ref, mk_ref,
                     mv_ref, out_ref, *, layer):
    t = pl.program_id(1)
    shift_m = bnd_ref[layer, 2]
    small = jnp.maximum(bnd_ref[layer, 1], shift_m) <= SHIFT_LIMIT
    srcs, pens = [], []
    for r in range(BAND_BLOCKS):
        blk = t - (BAND_BLOCKS - 1) + r
        srcs.append(jnp.maximum(blk, 0))
        pens.append(jnp.where(blk >= 0, 0.0, NEG_INF))

    def v_rows(h):
        return slice(h * V_ROWS, (h + 1) * V_ROWS)

    def normalised(pv):
        return pv[0:HEAD_DIM] / pv[HEAD_DIM:HEAD_DIM + 1]

    def pair_cols(h):
        return slice((h // 2) * PAIR, (h // 2 + 1) * PAIR)

    def attend(use_max):
        band_lg = [[_dot(kb_ref[0, srcs[r], :, pair_cols(h)],
                         qb_ref[0, h * PAIR:(h + 1) * PAIR, :])
                    + bias_ref[0, h, r * KB:(r + 1) * KB, :] + pens[r]
                    for r in range(BAND_BLOCKS)] for h in range(N_HEADS_B)]
        mem_lg = [_dot(mk_ref[0, :, pair_cols(h)],
                       qm_ref[0, h * PAIR:(h + 1) * PAIR, :]) - shift_m
                  for h in range(N_HEADS_M)]
        band_e = []
        for lgs in band_lg:
            if use_max:
                m = functools.reduce(jnp.maximum, [_col_max(lg) for lg in lgs])
                lgs = [lg - m for lg in lgs]
            band_e.append([jnp.exp2(lg).astype(BF16) for lg in lgs])
        mem_e = [jnp.exp2(lg - _col_max(lg) if use_max else lg).astype(BF16)
                 for lg in mem_lg]
        band_o = [normalised(functools.reduce(jnp.add, [
            _dot(vb_ref[0, srcs[r], v_rows(h), :], band_e[h][r])
            for r in range(BAND_BLOCKS)])) for h in range(N_HEADS_B)]
        mem_o = [normalised(_dot(mv_ref[0, v_rows(h), :], mem_e[h]))
                 for h in range(N_HEADS_M)]
        for p in range(N_HEADS_B // 2):
            out_ref[0, :, p * PAIR:(p + 1) * PAIR] = _finish_pair(
                band_o[2 * p:2 * p + 2])
        for p in range(N_HEADS_M // 2):
            out_ref[0, :, WIDTH_B + p * PAIR:WIDTH_B + (p + 1) * PAIR] = (
                _finish_pair(mem_o[2 * p:2 * p + 2]))

    @pl.when(small)
    def _():
        attend(False)

    @pl.when(jnp.logical_not(small))
    def _():
        attend(True)


def _band_mem_call(bounds, qb, kb, vb, bias, layer, qm, mk, mv):
    B, nkb = kb.shape[0], kb.shape[1]
    S = nkb * KB
    n_mem = mk.shape[1]

    def tile_t(rows):
        return pl.BlockSpec((1, rows, TQ), lambda b, t: (b, 0, t))

    def whole(a):
        return pl.BlockSpec((1,) + a.shape[1:], lambda b, t: (b, 0, 0, 0))

    return pl.pallas_call(
        functools.partial(_band_mem_kernel, layer=layer),
        grid=(B, S // TQ),
        in_specs=[
            pl.BlockSpec(memory_space=pltpu.SMEM),
            tile_t(N_HEADS_B * PAIR), whole(kb), whole(vb),
            pl.BlockSpec((1,) + bias.shape[1:], lambda b, t: (layer, 0, 0, 0)),
            tile_t(N_HEADS_M * PAIR),
            pl.BlockSpec((1, n_mem, WIDTH_M), lambda b, t: (b, 0, 0)),
            pl.BlockSpec((1, N_HEADS_M * V_ROWS, n_mem), lambda b, t: (b, 0, 0)),
        ],
        out_specs=pl.BlockSpec((1, TQ, WIDTH_B + WIDTH_M), lambda b, t: (b, t, 0)),
        out_shape=jax.ShapeDtypeStruct((B, S, WIDTH_B + WIDTH_M), BF16),
        compiler_params=pltpu.CompilerParams(
            dimension_semantics=("parallel", "parallel"),
            vmem_limit_bytes=VMEM_LIMIT),
        name="band_mem_attn",
    )(bounds, qb, kb, vb, bias, qm, mk, mv)


FF_CHUNK = 256


def _out_ffn_kernel(x_ref, oa_ref, obm_ref, wo_ref, gffn_ref, wgu_ref, wd_ref,
                    out_ref):
    attn = jnp.concatenate([oa_ref[...], obm_ref[...]], axis=1)
    x1 = x_ref[...] + _dot(attn, wo_ref[...])
    h = _row_rms(x1, gffn_ref[...]).astype(BF16)
    acc = x1
    for c in range(D_FF // FF_CHUNK):
        c0 = c * FF_CHUNK
        gate = _dot(h, wgu_ref[:, c0:c0 + FF_CHUNK])
        up = _dot(h, wgu_ref[:, D_FF + c0:D_FF + c0 + FF_CHUNK])
        act = gate * (1.0 / (1.0 + jnp.exp(-gate))) * up
        acc = acc + _dot(act.astype(BF16), wd_ref[c0:c0 + FF_CHUNK, :])
    out_ref[...] = acc


def _out_ffn_call(x2, oa2, obm2, wo, gffn, wgu, wd, layer):
    n = x2.shape[0]

    def rows(width):
        return pl.BlockSpec((TF, width), lambda i: (i, 0))

    def resident(a):
        return pl.BlockSpec((None,) + a.shape[1:], lambda i: (layer, 0, 0),
                            pipeline_mode=pl.Buffered(1))

    return pl.pallas_call(
        _out_ffn_kernel,
        grid=(n // TF,),
        in_specs=[rows(D_MODEL), rows(WIDTH_A), rows(WIDTH_B + WIDTH_M),
                  resident(wo), resident(gffn), resident(wgu), resident(wd)],
        out_specs=rows(D_MODEL),
        out_shape=jax.ShapeDtypeStruct((n, D_MODEL), F32),
        compiler_params=pltpu.CompilerParams(
            dimension_semantics=("parallel",), vmem_limit_bytes=VMEM_LIMIT),
        name="out_ffn",
    )(x2, oa2, obm2, wo, gffn, wgu, wd)


def kernel(x, mem, positions, g_mix, w_in, g_q_a, g_k_a, g_k_idx, g_q_b, g_k_b,
           rel_bias, g_q_m, g_k_m, g_mem, w_mem_kv, w_out, g_ffn, w_gate_up,
           w_down):
    B, S, D = x.shape
    depth = w_in.shape[0]

    w_in_t = jnp.swapaxes(w_in.astype(BF16), 1, 2)
    w_mem_t = jnp.swapaxes(w_mem_kv.astype(BF16), 1, 2)
    w_out_b = w_out.astype(BF16)
    w_gu_b = w_gate_up.astype(BF16)
    w_down_b = w_down.astype(BF16)
    gains = jnp.stack(
        [g_q_a, g_k_a, g_k_idx, g_q_b, g_k_b, g_q_m], axis=1)[..., None]

    inv_freq = jnp.power(
        ROPE_THETA, -jnp.arange(ROT_HALF, dtype=F32) / ROT_HALF)[:, None]
    pos3 = positions.reshape(B, 1, S)

    n_f = 2 * REL_CLIP
    fvec = jnp.concatenate(
        [rel_bias[..., 1:], jnp.broadcast_to(rel_bias[..., -1:],
                                             rel_bias.shape[:-1] + (n_f,))],
        axis=-1)[:, :, None, :]
    def amax(g):
        return jnp.max(jnp.abs(g), axis=-1)

    c = HEAD_DIM ** 0.5 * LOG2E * SHIFT_MARGIN
    bounds = jnp.stack(
        [c * amax(g_q_a) * amax(g_k_a),
         c * amax(g_q_b) * amax(g_k_b) + LOG2E * jnp.max(jnp.abs(rel_bias), (1, 2)),
         c * amax(g_q_m) * amax(g_k_m),
         jnp.zeros((depth,), F32)], axis=1)
    bias = _bias_call(bounds, fvec)

    for l in range(depth):
        qa, ka, va, qi, ki, wi, qb, kb, vb, qm = _proj_call(
            x, pos3, inv_freq, g_mix[l][None], w_in_t, l, gains[l])
        mk, mv = _memkv_call(mem, g_mem[l][None], w_mem_t, l, g_k_m[l][:, None])
        oa = _dsa_call(bounds, l, qa, ka, va, qi, ki, wi)
        obm = _band_mem_call(bounds, qb, kb, vb, bias, l, qm, mk, mv)
        x = _out_ffn_call(
            x.reshape(B * S, D), oa.reshape(B * S, WIDTH_A),
            obm.reshape(B * S, WIDTH_B + WIDTH_M), w_out_b, g_ffn[:, None],
            w_gu_b, w_down_b, l).reshape(B, S, D)
    return x
```

```python
import functools

import jax
import jax.numpy as jnp
from jax import lax
from jax.experimental import pallas as pl
from jax.experimental.pallas import tpu as pltpu

F32 = jnp.float32
BF16 = jnp.bfloat16
I32 = jnp.int32

D_MODEL = 1024
HEAD_DIM = 64
N_HEADS_A = 6
N_HEADS_B = 6
N_HEADS_M = 4
WIDTH_A = N_HEADS_A * HEAD_DIM
WIDTH_B = N_HEADS_B * HEAD_DIM
WIDTH_M = N_HEADS_M * HEAD_DIM
IDX_HEADS = 4
IDX_DIM = 64
ROT_DIM = HEAD_DIM // 4
ROT_HALF = ROT_DIM // 2
ROPE_THETA = 500000.0
CHUNK = 64
PREV_CHUNKS = 8
REL_CLIP = 256
TOPK_MAX = 256
D_FF = 2816
EPS = 1e-6
LOG2E = 1.4426950408889634

LANES = 128
SUBLANES = 8
PAIR = 2 * HEAD_DIM
V_ROWS = HEAD_DIM + 16
TM = 1024
TF = 512
TQ = 256
KB = 256
BAND_BLOCKS = 3
WI_ROWS = 8
INT_MIN = -(2 ** 31)
KEY_BITS = 32
NEG_INF = float("-inf")
MASKED = -1e30
SHIFT_LIMIT = 40.0
SHIFT_MARGIN = 1.02
VMEM_LIMIT = 56 * 1024 * 1024

R_QA = 0
R_KA = R_QA + WIDTH_A
R_VA = R_KA + WIDTH_A
R_QI = R_VA + WIDTH_A
R_KI = R_QI + IDX_HEADS * IDX_DIM
R_WI = R_KI + IDX_DIM
R_QB = R_WI + IDX_HEADS
R_KB = R_QB + WIDTH_B
R_VB = R_KB + WIDTH_B
R_QM = R_VB + WIDTH_B
R_END = R_QM + WIDTH_M

NT_DIMS = (((1,), (1,)), ((), ()))


def _dot(a, b):
    return jnp.dot(a, b, preferred_element_type=F32)


def _dot_nt(a, b):
    return lax.dot_general(a, b, NT_DIMS, preferred_element_type=F32)


def _chunk_of(pos):
    return jnp.right_shift(pos, CHUNK.bit_length() - 1)


def _tree_sum(xs):
    while len(xs) > 1:
        xs = [a + b for a, b in zip(xs[0::2], xs[1::2])] + xs[len(xs) & ~1:]
    return xs[0]


def _as_i32(v):
    return v - (1 << 32) if v >= (1 << 31) else v


def _bit_transpose32(rows):
    rows = list(rows)
    j, m = 16, 0x0000FFFF
    while j:
        k = 0
        while k < 32:
            t = ((rows[k] ^ lax.shift_right_logical(rows[k + j], jnp.int32(j)))
                 & jnp.int32(_as_i32(m)))
            rows[k] = rows[k] ^ t
            rows[k + j] = rows[k + j] ^ lax.shift_left(t, jnp.int32(j))
            k = (k + j + 1) & ~j
        j >>= 1
        if j:
            m = (m ^ (m << j)) & 0xFFFFFFFF
    return rows


def _row_rms(x, g):
    ms = jnp.mean(x * x, axis=-1, keepdims=True)
    return x * lax.rsqrt(ms + EPS) * g


def _head_norm_t(blk, g_col):
    ms = jnp.mean(blk * blk, axis=0, keepdims=True)
    return blk * lax.rsqrt(ms + EPS) * g_col


def _rope_t(y, cos, sin):
    y1 = y[0:ROT_HALF]
    y2 = y[ROT_HALF:ROT_DIM]
    return jnp.concatenate(
        [y1 * cos - y2 * sin, y2 * cos + y1 * sin, y[ROT_DIM:]], axis=0)


def _with_ones(v_heads, n_heads):
    ones = jnp.ones((V_ROWS - HEAD_DIM, v_heads.shape[1]), v_heads.dtype)
    parts = []
    for i in range(n_heads):
        parts += [v_heads[i * HEAD_DIM:(i + 1) * HEAD_DIM], ones]
    return jnp.concatenate(parts, axis=0)


def _pair_slot(y, odd):
    z = jnp.zeros_like(y)
    return jnp.concatenate([z, y] if odd else [y, z], axis=0)


def _proj_kernel(x_ref, pos_ref, invf_ref, gmix_ref, wt_ref, gains_ref,
                 qa_ref, ka_ref, va_ref, qi_ref, ki_ref, wi_ref,
                 qb_ref, kb_ref, vb_ref, qm_ref):
    h = _row_rms(x_ref[0], gmix_ref[...]).astype(BF16)
    ang = invf_ref[...] * pos_ref[0].astype(F32)
    cos = jnp.cos(ang)
    sin = jnp.sin(ang)
    q_scale = HEAD_DIM ** -0.5 * LOG2E
    n_blk = TM // KB

    def proj_t(r0, r1):
        return _dot_nt(wt_ref[r0:r1, :], h)

    def gain(i):
        return gains_ref[i]

    def head(t, i):
        return t[i * HEAD_DIM:(i + 1) * HEAD_DIM]

    def store_q(ref, t, n_heads, g, rope):
        for i in range(n_heads):
            y = head(t, i)
            if g is not None:
                y = _head_norm_t(y, g)
            if rope:
                y = _rope_t(y, cos, sin)
            ref[0, i * PAIR:(i + 1) * PAIR, :] = _pair_slot(
                y * q_scale, i % 2 == 1).astype(BF16)

    def store_k(ref, t, n_heads, g, rope):
        for p in range(n_heads // 2):
            ys = []
            for i in (2 * p, 2 * p + 1):
                y = _head_norm_t(head(t, i), g)
                ys.append(_rope_t(y, cos, sin) if rope else y)
            blk = jnp.concatenate(ys, axis=0).T.astype(BF16)
            for j in range(n_blk):
                ref[0, j, :, p * PAIR:(p + 1) * PAIR] = blk[j * KB:(j + 1) * KB]

    def store_v(ref, t, n_heads):
        tb = _with_ones(t.astype(BF16), n_heads)
        for j in range(n_blk):
            ref[0, j] = tb[:, j * KB:(j + 1) * KB]

    store_q(qa_ref, proj_t(R_QA, R_KA), N_HEADS_A, gain(0), True)
    store_k(ka_ref, proj_t(R_KA, R_VA), N_HEADS_A, gain(1), True)
    store_v(va_ref, proj_t(R_VA, R_QI), N_HEADS_A)

    t = proj_t(R_QI, R_WI + WI_ROWS)
    for i in range(IDX_HEADS):
        y = _rope_t(head(t, i), cos, sin) * (IDX_DIM ** -0.5)
        qi_ref[0, i * PAIR:(i + 1) * PAIR, :] = _pair_slot(y, False).astype(BF16)
    ki = _rope_t(_head_norm_t(head(t, IDX_HEADS), gain(2)), cos, sin)
    ki_blk = _pair_slot(ki, False).T.astype(BF16)
    for j in range(n_blk):
        ki_ref[0, j] = ki_blk[j * KB:(j + 1) * KB]
    wi_ref[0] = t[R_WI - R_QI:R_WI - R_QI + WI_ROWS] * (IDX_HEADS ** -0.5)

    store_q(qb_ref, proj_t(R_QB, R_KB), N_HEADS_B, gain(3), False)
    store_k(kb_ref, proj_t(R_KB, R_VB), N_HEADS_B, gain(4), False)
    store_v(vb_ref, proj_t(R_VB, R_QM), N_HEADS_B)
    store_q(qm_ref, proj_t(R_QM, R_END), N_HEADS_M, gain(5), False)


def _proj_call(x, pos3, invf, gmix, wt, layer, gains):
    B, S, _ = x.shape
    nkb = S // KB
    n_blk = TM // KB

    def tok_t(rows):
        return pl.BlockSpec((1, rows, TM), lambda b, i: (b, 0, i))

    def full(shape):
        return pl.BlockSpec(shape, lambda b, i: (0,) * len(shape))

    def krow(width):
        return pl.BlockSpec((1, n_blk, KB, width), lambda b, i: (b, i, 0, 0))

    def vt(rows):
        return pl.BlockSpec((1, n_blk, rows, KB), lambda b, i: (b, i, 0, 0))

    out_shape = (
        jax.ShapeDtypeStruct((B, N_HEADS_A * PAIR, S), BF16),
        jax.ShapeDtypeStruct((B, nkb, KB, WIDTH_A), BF16),
        jax.ShapeDtypeStruct((B, nkb, N_HEADS_A * V_ROWS, KB), BF16),
        jax.ShapeDtypeStruct((B, IDX_HEADS * PAIR, S), BF16),
        jax.ShapeDtypeStruct((B, nkb, KB, PAIR), BF16),
        jax.ShapeDtypeStruct((B, WI_ROWS, S), F32),
        jax.ShapeDtypeStruct((B, N_HEADS_B * PAIR, S), BF16),
        jax.ShapeDtypeStruct((B, nkb, KB, WIDTH_B), BF16),
        jax.ShapeDtypeStruct((B, nkb, N_HEADS_B * V_ROWS, KB), BF16),
        jax.ShapeDtypeStruct((B, N_HEADS_M * PAIR, S), BF16),
    )
    out_specs = (
        tok_t(N_HEADS_A * PAIR), krow(WIDTH_A), vt(N_HEADS_A * V_ROWS),
        tok_t(IDX_HEADS * PAIR), krow(PAIR), tok_t(WI_ROWS),
        tok_t(N_HEADS_B * PAIR), krow(WIDTH_B), vt(N_HEADS_B * V_ROWS),
        tok_t(N_HEADS_M * PAIR),
    )
    return pl.pallas_call(
        _proj_kernel,
        grid=(B, S // TM),
        in_specs=[
            pl.BlockSpec((1, TM, D_MODEL), lambda b, i: (b, i, 0)),
            pl.BlockSpec((1, 1, TM), lambda b, i: (b, 0, i)),
            full((ROT_HALF, 1)),
            full((1, D_MODEL)),
            pl.BlockSpec((None, R_END, D_MODEL), lambda b, i: (layer, 0, 0)),
            full((6, HEAD_DIM, 1)),
        ],
        out_specs=out_specs,
        out_shape=out_shape,
        compiler_params=pltpu.CompilerParams(
            dimension_semantics=("parallel", "parallel"),
            vmem_limit_bytes=VMEM_LIMIT),
        name="in_proj",
    )(x, pos3, invf, gmix, wt, gains)


def _memkv_kernel(mem_ref, gmem_ref, wt_ref, gk_ref, mk_ref, mv_ref):
    hm = _row_rms(mem_ref[0], gmem_ref[...]).astype(BF16)
    kt = _dot_nt(wt_ref[0:WIDTH_M, :], hm)
    for p in range(N_HEADS_M // 2):
        ys = [_head_norm_t(kt[i * HEAD_DIM:(i + 1) * HEAD_DIM], gk_ref[...])
              for i in (2 * p, 2 * p + 1)]
        mk_ref[0, :, p * PAIR:(p + 1) * PAIR] = (
            jnp.concatenate(ys, axis=0).T.astype(BF16))
    mv_ref[0] = _with_ones(
        _dot_nt(wt_ref[WIDTH_M:2 * WIDTH_M, :], hm).astype(BF16), N_HEADS_M)


def _memkv_call(mem, gmem, wt, layer, gk):
    B, n_mem, _ = mem.shape
    return pl.pallas_call(
        _memkv_kernel,
        grid=(B,),
        in_specs=[
            pl.BlockSpec((1, n_mem, D_MODEL), lambda b: (b, 0, 0)),
            pl.BlockSpec((1, D_MODEL), lambda b: (0, 0)),
            pl.BlockSpec((None, 2 * WIDTH_M, D_MODEL), lambda b: (layer, 0, 0)),
            pl.BlockSpec((HEAD_DIM, 1), lambda b: (0, 0)),
        ],
        out_specs=(
            pl.BlockSpec((1, n_mem, WIDTH_M), lambda b: (b, 0, 0)),
            pl.BlockSpec((1, N_HEADS_M * V_ROWS, n_mem), lambda b: (b, 0, 0)),
        ),
        out_shape=(
            jax.ShapeDtypeStruct((B, n_mem, WIDTH_M), BF16),
            jax.ShapeDtypeStruct((B, N_HEADS_M * V_ROWS, n_mem), BF16),
        ),
        compiler_params=pltpu.CompilerParams(
            dimension_semantics=("parallel",), vmem_limit_bytes=VMEM_LIMIT),
        name="mem_kv",
    )(mem, gmem, wt, gk)


def _bias_kernel(bnd_ref, f_ref, out_ref):
    n_keys = BAND_BLOCKS * KB
    width = f_ref.shape[-1]
    base = jnp.broadcast_to(f_ref[0, 0], (n_keys, width))
    rolled = pltpu.roll(base, width - (n_keys - 1), 1, stride=1, stride_axis=0)
    tab = rolled[:, 0:TQ]
    key_c = _chunk_of(lax.broadcasted_iota(I32, (n_keys, TQ), 0))
    qry_c = _chunk_of(lax.broadcasted_iota(I32, (n_keys, TQ), 1)
                      + (BAND_BLOCKS - 1) * KB)
    ok = (key_c <= qry_c) & (key_c >= qry_c - PREV_CHUNKS)
    shift = bnd_ref[pl.program_id(0), 1]
    out_ref[0, 0] = jnp.where(ok, tab * LOG2E - shift, NEG_INF)


def _bias_call(bounds, fvec):
    depth, n_heads, _, width = fvec.shape
    n_keys = BAND_BLOCKS * KB
    return pl.pallas_call(
        _bias_kernel,
        grid=(depth, n_heads),
        in_specs=[pl.BlockSpec(memory_space=pltpu.SMEM),
                  pl.BlockSpec((1, 1, 1, width), lambda l, h: (l, h, 0, 0))],
        out_specs=pl.BlockSpec((1, 1, n_keys, TQ), lambda l, h: (l, h, 0, 0)),
        out_shape=jax.ShapeDtypeStruct((depth, n_heads, n_keys, TQ), F32),
        compiler_params=pltpu.CompilerParams(
            dimension_semantics=("parallel", "parallel")),
        name="band_bias",
    )(bounds, fvec)


def _col_max(x):
    return jnp.max(x, axis=0, keepdims=True)


def _col_sum(x):
    return jnp.sum(x, axis=0, keepdims=True)


def _finish_pair(outs):
    return jnp.concatenate(outs, axis=0).T.astype(BF16)


def _dsa_kernel(bnd_ref, qa_ref, ka_ref, va_ref, qi_ref, ki_ref, wi_ref, out_ref,
                keys_ref, planes_ref, mask_ref, acc_ref, *, layer):
    refs = (bnd_ref, qa_ref, ka_ref, va_ref, qi_ref, ki_ref, wi_ref, out_ref,
            keys_ref, planes_ref, mask_ref, acc_ref)
    for n in range(1, keys_ref.shape[0] + 1):
        pl.when(pl.program_id(1) == n - 1)(
            functools.partial(_dsa_tile, n, layer, *refs))


def _static_loop(n, body, carry):
    for j in range(n):
        carry = body(j, carry)
    return carry


def _dsa_tile(nkb, layer, bnd_ref, qa_ref, ka_ref, va_ref, qi_ref, ki_ref, wi_ref,
              out_ref, keys_ref, planes_ref, mask_ref, acc_ref):
    t = nkb - 1
    shift = bnd_ref[layer, 0]
    n_pair = nkb // 2
    odd = nkb % 2 == 1
    qry_c = _chunk_of(t * TQ + lax.broadcasted_iota(I32, (KB, TQ), 1))
    key_c0 = _chunk_of(lax.broadcasted_iota(I32, (KB, TQ), 0))

    def score_blocks(kbs):
        dss = [[_dot(ki_ref[0, kb], qi_ref[0, i * PAIR:(i + 1) * PAIR, :])
                for i in range(IDX_HEADS)] for kb in kbs]
        for kb, ds in zip(kbs, dss):
            s = jnp.zeros((KB, TQ), F32)
            for i in range(IDX_HEADS):
                s = s + jnp.maximum(ds[i], 0.0) * wi_ref[0, i:i + 1, :]
            s = jnp.where(s == 0.0, 0.0, s)
            bits = lax.bitcast_convert_type(s, I32)
            key = jnp.where(bits >= 0, bits, bits ^ jnp.int32(0x7FFFFFFF))
            allowed = (key_c0 + kb * (KB // CHUNK)) <= qry_c
            key = jnp.where(allowed, key, jnp.int32(INT_MIN))
            keys_ref[kb] = key
            u = key ^ jnp.int32(INT_MIN)
            cols = _bit_transpose32(
                [u[j * SUBLANES:(j + 1) * SUBLANES] for j in range(KEY_BITS)])
            for b in range(KEY_BITS):
                planes_ref[b, kb] = cols[KEY_BITS - 1 - b]

    def score_pair(j, carry):
        score_blocks([2 * j, 2 * j + 1])
        return carry

    _static_loop(n_pair, score_pair, 0)
    if odd:
        score_blocks([nkb - 1])

    def col_count(words):
        return _col_sum(_tree_sum([lax.population_count(w) for w in words]))

    def bit(i, carry):
        active, rank, thr_u = carry
        b = KEY_BITS - 1 - i
        planes = planes_ref[b, 0:nkb]
        ones = [active[kb] & planes[kb] for kb in range(nkb)]
        c1 = col_count(ones)
        take = c1 >= rank
        active = tuple(jnp.where(take, ones[kb], active[kb] ^ ones[kb])
                       for kb in range(nkb))
        rank = jnp.where(take, rank, rank - c1)
        thr_u = jnp.where(take, thr_u | lax.shift_left(jnp.int32(1), b), thr_u)
        return active, rank, thr_u

    active0 = tuple(jnp.full((SUBLANES, TQ), -1, I32) for _ in range(nkb))
    active, rank, thr_u = lax.fori_loop(
        0, KEY_BITS, bit,
        (active0, jnp.full((1, TQ), TOPK_MAX, I32), jnp.zeros((1, TQ), I32)))
    thr = thr_u ^ jnp.int32(INT_MIN)
    n_ge = TOPK_MAX - rank + col_count(active)
    tie = (n_ge > TOPK_MAX) & (thr > INT_MIN)
    any_tie = jnp.max(jnp.where(tie, 1.0, 0.0)) > 0.0

    @pl.when(jnp.logical_not(any_tie))
    def _():
        floor = jnp.maximum(thr, jnp.int32(INT_MIN + 1))

        def blk(kb, carry):
            mask_ref[kb] = jnp.where(keys_ref[kb] >= floor, -shift, MASKED)
            return carry

        _static_loop(nkb, blk, 0)

    @pl.when(any_tie)
    def _():
        lower = (lax.broadcasted_iota(I32, (KB, KB), 1)
                 < lax.broadcasted_iota(I32, (KB, KB), 0))
        ltri = jnp.where(lower, 1.0, 0.0).astype(BF16)
        need = rank.astype(F32)

        def blk(kb, seen):
            k = keys_ref[kb]
            eq = jnp.where(k == thr, 1.0, 0.0)
            before = _dot(ltri, eq.astype(BF16)) + seen
            keep = (k > thr) | ((k == thr) & (before < need))
            keep = keep & (k > INT_MIN)
            mask_ref[kb] = jnp.where(keep, -shift, MASKED)
            return seen + _col_sum(eq)

        _static_loop(nkb, blk, jnp.zeros((1, TQ), F32))

    acc_ref[...] = jnp.zeros_like(acc_ref)
    heads = range(N_HEADS_A)

    def logits(kbs):
        masks = [mask_ref[kb] for kb in kbs]
        return [[_dot(ka_ref[0, kb, :, (h // 2) * PAIR:(h // 2 + 1) * PAIR],
                      qa_ref[0, h * PAIR:(h + 1) * PAIR, :]) + mask
                 for kb, mask in zip(kbs, masks)] for h in heads]

    def weighted_values(h, kbs, es):
        return functools.reduce(jnp.add, [
            _dot(va_ref[0, kb, h * V_ROWS:(h + 1) * V_ROWS, :], e.astype(BF16))
            for kb, e in zip(kbs, es)])

    @pl.when(shift <= SHIFT_LIMIT)
    def _():
        def blocks(kbs):
            lgs = logits(kbs)
            for h in heads:
                acc_ref[h] += weighted_values(
                    h, kbs, [jnp.exp2(lg) for lg in lgs[h]])

        def pair(j, carry):
            blocks([2 * j, 2 * j + 1])
            return carry

        _static_loop(n_pair, pair, 0)
        if odd:
            blocks([nkb - 1])

    @pl.when(shift > SHIFT_LIMIT)
    def _():
        def blocks(kbs, ms):
            lgs = logits(kbs)
            ms_new = []
            for h in heads:
                m_new = functools.reduce(
                    jnp.maximum, [ms[h]] + [_col_max(lg) for lg in lgs[h]])
                acc_ref[h] = (acc_ref[h] * jnp.exp2(ms[h] - m_new)
                              + weighted_values(
                                  h, kbs, [jnp.exp2(lg - m_new) for lg in lgs[h]]))
                ms_new.append(m_new)
            return tuple(ms_new)

        def pair(j, ms):
            return blocks([2 * j, 2 * j + 1], ms)

        m0 = tuple(jnp.full((1, TQ), MASKED, F32) for _ in heads)
        ms = _static_loop(n_pair, pair, m0)
        if odd:
            blocks([nkb - 1], ms)

    def head_out(h):
        return acc_ref[h, 0:HEAD_DIM] / acc_ref[h, HEAD_DIM:HEAD_DIM + 1]

    for p in range(N_HEADS_A // 2):
        outs = [head_out(h) for h in (2 * p, 2 * p + 1)]
        out_ref[0, :, p * PAIR:(p + 1) * PAIR] = _finish_pair(outs)


def _dsa_call(bounds, layer, qa, ka, va, qi, ki, wi):
    B, nkb = ka.shape[0], ka.shape[1]
    S = nkb * KB

    def tile_t(rows):
        return pl.BlockSpec((1, rows, TQ), lambda b, t: (b, 0, t))

    def whole(a):
        return pl.BlockSpec((1,) + a.shape[1:], lambda b, t: (b, 0, 0, 0))

    return pl.pallas_call(
        functools.partial(_dsa_kernel, layer=layer),
        grid=(B, S // TQ),
        in_specs=[pl.BlockSpec(memory_space=pltpu.SMEM), tile_t(N_HEADS_A * PAIR), whole(ka), whole(va),
                  tile_t(IDX_HEADS * PAIR), whole(ki), tile_t(WI_ROWS)],
        out_specs=pl.BlockSpec((1, TQ, WIDTH_A), lambda b, t: (b, t, 0)),
        out_shape=jax.ShapeDtypeStruct((B, S, WIDTH_A), BF16),
        scratch_shapes=[
            pltpu.VMEM((nkb, KB, TQ), I32),
            pltpu.VMEM((KEY_BITS, nkb, SUBLANES, TQ), I32),
            pltpu.VMEM((nkb, KB, TQ), F32),
            pltpu.VMEM((N_HEADS_A, V_ROWS, TQ), F32),
        ],
        compiler_params=pltpu.CompilerParams(
            dimension_semantics=("parallel", "arbitrary"),
            vmem_limit_bytes=VMEM_LIMIT),
        name="dsa_attn",
    )(bounds, qa, ka, va, qi, ki, wi)


def _band_mem_kernel(bnd_ref, qb_ref, kb_ref, vb_ref, bias_ref, qm_ref, mk_ref,
                     mv_ref, out_ref, *, layer):
    t = pl.program_id(1)
    shift_m = bnd_ref[layer, 2]
    small = jnp.maximum(bnd_ref[layer, 1], shift_m) <= SHIFT_LIMIT
    srcs, pens = [], []
    for r in range(BAND_BLOCKS):
        blk = t - (BAND_BLOCKS - 1) + r
        srcs.append(jnp.maximum(blk, 0))
        pens.append(jnp.where(blk >= 0, 0.0, NEG_INF))

    def v_rows(h):
        return slice(h * V_ROWS, (h + 1) * V_ROWS)

    def normalised(pv):
        return pv[0:HEAD_DIM] / pv[HEAD_DIM:HEAD_DIM + 1]

    def pair_cols(h):
        return slice((h // 2) * PAIR, (h // 2 + 1) * PAIR)

    def attend(use_max):
        band_lg = [[_dot(kb_ref[0, srcs[r], :, pair_cols(h)],
                         qb_ref[0, h * PAIR:(h + 1) * PAIR, :])
                    + bias_ref[0, h, r * KB:(r + 1) * KB, :] + pens[r]
                    for r in range(BAND_BLOCKS)] for h in range(N_HEADS_B)]
        mem_lg = [_dot(mk_ref[0, :, pair_cols(h)],
                       qm_ref[0, h * PAIR:(h + 1) * PAIR, :]) - shift_m
                  for h in range(N_HEADS_M)]
        band_e = []
        for lgs in band_lg:
            if use_max:
                m = functools.reduce(jnp.maximum, [_col_max(lg) for lg in lgs])
                lgs = [lg - m for lg in lgs]
            band_e.append([jnp.exp2(lg).astype(BF16) for lg in lgs])
        mem_e = [jnp.exp2(lg - _col_max(lg) if use_max else lg).astype(BF16)
                 for lg in mem_lg]
        band_o = [normalised(functools.reduce(jnp.add, [
            _dot(vb_ref[0, srcs[r], v_rows(h), :], band_e[h][r])
            for r in range(BAND_BLOCKS)])) for h in range(N_HEADS_B)]
        mem_o = [normalised(_dot(mv_ref[0, v_rows(h), :], mem_e[h]))
                 for h in range(N_HEADS_M)]
        for p in range(N_HEADS_B // 2):
            out_ref[0, :, p * PAIR:(p + 1) * PAIR] = _finish_pair(
                band_o[2 * p:2 * p + 2])
        for p in range(N_HEADS_M // 2):
            out_ref[0, :, WIDTH_B + p * PAIR:WIDTH_B + (p + 1) * PAIR] = (
                _finish_pair(mem_o[2 * p:2 * p + 2]))

    @pl.when(small)
    def _():
        attend(False)

    @pl.when(jnp.logical_not(small))
    def _():
        attend(True)


def _band_mem_call(bounds, qb, kb, vb, bias, layer, qm, mk, mv):
    B, nkb = kb.shape[0], kb.shape[1]
    S = nkb * KB
    n_mem = mk.shape[1]

    def tile_t(rows):
        return pl.BlockSpec((1, rows, TQ), lambda b, t: (b, 0, t))

    def whole(a):
        return pl.BlockSpec((1,) + a.shape[1:], lambda b, t: (b, 0, 0, 0))

    return pl.pallas_call(
        functools.partial(_band_mem_kernel, layer=layer),
        grid=(B, S // TQ),
        in_specs=[
            pl.BlockSpec(memory_space=pltpu.SMEM),
            tile_t(N_HEADS_B * PAIR), whole(kb), whole(vb),
            pl.BlockSpec((1,) + bias.shape[1:], lambda b, t: (layer, 0, 0, 0)),
            tile_t(N_HEADS_M * PAIR),
            pl.BlockSpec((1, n_mem, WIDTH_M), lambda b, t: (b, 0, 0)),
            pl.BlockSpec((1, N_HEADS_M * V_ROWS, n_mem), lambda b, t: (b, 0, 0)),
        ],
        out_specs=pl.BlockSpec((1, TQ, WIDTH_B + WIDTH_M), lambda b, t: (b, t, 0)),
        out_shape=jax.ShapeDtypeStruct((B, S, WIDTH_B + WIDTH_M), BF16),
        compiler_params=pltpu.CompilerParams(
            dimension_semantics=("parallel", "parallel"),
            vmem_limit_bytes=VMEM_LIMIT),
        name="band_mem_attn",
    )(bounds, qb, kb, vb, bias, qm, mk, mv)


FF_CHUNK = 256


def _out_ffn_kernel(x_ref, oa_ref, obm_ref, wo_ref, gffn_ref, wgu_ref, wd_ref,
                    out_ref):
    attn = jnp.concatenate([oa_ref[...], obm_ref[...]], axis=1)
    x1 = x_ref[...] + _dot(attn, wo_ref[...])
    h = _row_rms(x1, gffn_ref[...]).astype(BF16)
    acc = x1
    for c in range(D_FF // FF_CHUNK):
        c0 = c * FF_CHUNK
        gate = _dot(h, wgu_ref[:, c0:c0 + FF_CHUNK])
        up = _dot(h, wgu_ref[:, D_FF + c0:D_FF + c0 + FF_CHUNK])
        act = gate * (1.0 / (1.0 + jnp.exp(-gate))) * up
        acc = acc + _dot(act.astype(BF16), wd_ref[c0:c0 + FF_CHUNK, :])
    out_ref[...] = acc


def _out_ffn_call(x2, oa2, obm2, wo, gffn, wgu, wd, layer):
    n = x2.shape[0]

    def rows(width):
        return pl.BlockSpec((TF, width), lambda i: (i, 0))

    def resident(a):
        return pl.BlockSpec((None,) + a.shape[1:], lambda i: (layer, 0, 0),
                            pipeline_mode=pl.Buffered(1))

    return pl.pallas_call(
        _out_ffn_kernel,
        grid=(n // TF,),
        in_specs=[rows(D_MODEL), rows(WIDTH_A), rows(WIDTH_B + WIDTH_M),
                  resident(wo), resident(gffn), resident(wgu), resident(wd)],
        out_specs=rows(D_MODEL),
        out_shape=jax.ShapeDtypeStruct((n, D_MODEL), F32),
        compiler_params=pltpu.CompilerParams(
            dimension_semantics=("parallel",), vmem_limit_bytes=VMEM_LIMIT),
        name="out_ffn",
    )(x2, oa2, obm2, wo, gffn, wgu, wd)


def kernel(x, mem, positions, g_mix, w_in, g_q_a, g_k_a, g_k_idx, g_q_b, g_k_b,
           rel_bias, g_q_m, g_k_m, g_mem, w_mem_kv, w_out, g_ffn, w_gate_up,
           w_down):
    B, S, D = x.shape
    depth = w_in.shape[0]

    w_in_t = jnp.swapaxes(w_in.astype(BF16), 1, 2)
    w_mem_t = jnp.swapaxes(w_mem_kv.astype(BF16), 1, 2)
    w_out_b = w_out.astype(BF16)
    w_gu_b = w_gate_up.astype(BF16)
    w_down_b = w_down.astype(BF16)
    gains = jnp.stack(
        [g_q_a, g_k_a, g_k_idx, g_q_b, g_k_b, g_q_m], axis=1)[..., None]

    inv_freq = jnp.power(
        ROPE_THETA, -jnp.arange(ROT_HALF, dtype=F32) / ROT_HALF)[:, None]
    pos3 = positions.reshape(B, 1, S)

    n_f = 2 * REL_CLIP
    fvec = jnp.concatenate(
        [rel_bias[..., 1:], jnp.broadcast_to(rel_bias[..., -1:],
                                             rel_bias.shape[:-1] + (n_f,))],
        axis=-1)[:, :, None, :]
    def amax(g):
        return jnp.max(jnp.abs(g), axis=-1)

    c = HEAD_DIM ** 0.5 * LOG2E * SHIFT_MARGIN
    bounds = jnp.stack(
        [c * amax(g_q_a) * amax(g_k_a),
         c * amax(g_q_b) * amax(g_k_b) + LOG2E * jnp.max(jnp.abs(rel_bias), (1, 2)),
         c * amax(g_q_m) * amax(g_k_m),
         jnp.zeros((depth,), F32)], axis=1)
    bias = _bias_call(bounds, fvec)

    for l in range(depth):
        qa, ka, va, qi, ki, wi, qb, kb, vb, qm = _proj_call(
            x, pos3, inv_freq, g_mix[l][None], w_in_t, l, gains[l])
        mk, mv = _memkv_call(mem, g_mem[l][None], w_mem_t, l, g_k_m[l][:, None])
        oa = _dsa_call(bounds, l, qa, ka, va, qi, ki, wi)
        obm = _band_mem_call(bounds, qb, kb, vb, bias, l, qm, mk, mv)
        x = _out_ffn_call(
            x.reshape(B * S, D), oa.reshape(B * S, WIDTH_A),
            obm.reshape(B * S, WIDTH_B + WIDTH_M), w_out_b, g_ffn[:, None],
            w_gu_b, w_down_b, l).reshape(B, S, D)
    return x
```

```python
import functools

import jax
import jax.numpy as jnp
from jax import lax
from jax.experimental import pallas as pl
from jax.experimental.pallas import tpu as pltpu

F32 = jnp.float32
BF16 = jnp.bfloat16
I32 = jnp.int32

D_MODEL = 1024
HEAD_DIM = 64
N_HEADS_A = 6
N_HEADS_B = 6
N_HEADS_M = 4
WIDTH_A = N_HEADS_A * HEAD_DIM
WIDTH_B = N_HEADS_B * HEAD_DIM
WIDTH_M = N_HEADS_M * HEAD_DIM
IDX_HEADS = 4
IDX_DIM = 64
ROT_DIM = HEAD_DIM // 4
ROT_HALF = ROT_DIM // 2
ROPE_THETA = 500000.0
CHUNK = 64
PREV_CHUNKS = 8
REL_CLIP = 256
TOPK_MAX = 256
D_FF = 2816
EPS = 1e-6
LOG2E = 1.4426950408889634

LANES = 128
SUBLANES = 8
PAIR = 2 * HEAD_DIM
V_ROWS = HEAD_DIM + 16
TM = 1024
TF = 512
TQ = 256
KB = 256
BAND_BLOCKS = 3
WI_ROWS = 8
INT_MIN = -(2 ** 31)
KEY_BITS = 32
NEG_INF = float("-inf")
MASKED = -1e30
SHIFT_LIMIT = 40.0
SHIFT_MARGIN = 1.02
VMEM_LIMIT = 56 * 1024 * 1024

R_QA = 0
R_KA = R_QA + WIDTH_A
R_VA = R_KA + WIDTH_A
R_QI = R_VA + WIDTH_A
R_KI = R_QI + IDX_HEADS * IDX_DIM
R_WI = R_KI + IDX_DIM
R_QB = R_WI + IDX_HEADS
R_KB = R_QB + WIDTH_B
R_VB = R_KB + WIDTH_B
R_QM = R_VB + WIDTH_B
R_END = R_QM + WIDTH_M

NT_DIMS = (((1,), (1,)), ((), ()))


def _dot(a, b):
    return jnp.dot(a, b, preferred_element_type=F32)


def _dot_nt(a, b):
    return lax.dot_general(a, b, NT_DIMS, preferred_element_type=F32)


def _chunk_of(pos):
    return jnp.right_shift(pos, CHUNK.bit_length() - 1)


def _tree_sum(xs):
    while len(xs) > 1:
        xs = [a + b for a, b in zip(xs[0::2], xs[1::2])] + xs[len(xs) & ~1:]
    return xs[0]


def _as_i32(v):
    return v - (1 << 32) if v >= (1 << 31) else v


def _bit_transpose32(rows):
    rows = list(rows)
    j, m = 16, 0x0000FFFF
    while j:
        k = 0
        while k < 32:
            t = ((rows[k] ^ lax.shift_right_logical(rows[k + j], jnp.int32(j)))
                 & jnp.int32(_as_i32(m)))
            rows[k] = rows[k] ^ t
            rows[k + j] = rows[k + j] ^ lax.shift_left(t, jnp.int32(j))
            k = (k + j + 1) & ~j
        j >>= 1
        if j:
            m = (m ^ (m << j)) & 0xFFFFFFFF
    return rows


def _row_rms(x, g):
    ms = jnp.mean(x * x, axis=-1, keepdims=True)
    return x * lax.rsqrt(ms + EPS) * g


def _head_norm_t(blk, g_col):
    ms = jnp.mean(blk * blk, axis=0, keepdims=True)
    return blk * lax.rsqrt(ms + EPS) * g_col


def _rope_t(y, cos, sin):
    y1 = y[0:ROT_HALF]
    y2 = y[ROT_HALF:ROT_DIM]
    return jnp.concatenate(
        [y1 * cos - y2 * sin, y2 * cos + y1 * sin, y[ROT_DIM:]], axis=0)


def _with_ones(v_heads, n_heads):
    ones = jnp.ones((V_ROWS - HEAD_DIM, v_heads.shape[1]), v_heads.dtype)
    parts = []
    for i in range(n_heads):
        parts += [v_heads[i * HEAD_DIM:(i + 1) * HEAD_DIM], ones]
    return jnp.concatenate(parts, axis=0)


def _pair_slot(y, odd):
    z = jnp.zeros_like(y)
    return jnp.concatenate([z, y] if odd else [y, z], axis=0)


def _proj_kernel(x_ref, pos_ref, invf_ref, gmix_ref, wt_ref, gains_ref,
                 qa_ref, ka_ref, va_ref, qi_ref, ki_ref, wi_ref,
                 qb_ref, kb_ref, vb_ref, qm_ref):
    h = _row_rms(x_ref[0], gmix_ref[...]).astype(BF16)
    ang = invf_ref[...] * pos_ref[0].astype(F32)
    cos = jnp.cos(ang)
    sin = jnp.sin(ang)
    q_scale = HEAD_DIM ** -0.5 * LOG2E
    n_blk = TM // KB

    def proj_t(r0, r1):
        return _dot_nt(wt_ref[r0:r1, :], h)

    def gain(i):
        return gains_ref[i]

    def head(t, i):
        return t[i * HEAD_DIM:(i + 1) * HEAD_DIM]

    def store_q(ref, t, n_heads, g, rope):
        for i in range(n_heads):
            y = head(t, i)
            if g is not None:
                y = _head_norm_t(y, g)
            if rope:
                y = _rope_t(y, cos, sin)
            ref[0, i * PAIR:(i + 1) * PAIR, :] = _pair_slot(
                y * q_scale, i % 2 == 1).astype(BF16)

    def store_k(ref, t, n_heads, g, rope):
        for p in range(n_heads // 2):
            ys = []
            for i in (2 * p, 2 * p + 1):
                y = _head_norm_t(head(t, i), g)
                ys.append(_rope_t(y, cos, sin) if rope else y)
            blk = jnp.concatenate(ys, axis=0).T.astype(BF16)
            for j in range(n_blk):
                ref[0, j, :, p * PAIR:(p + 1) * PAIR] = blk[j * KB:(j + 1) * KB]

    def store_v(ref, t, n_heads):
        tb = _with_ones(t.astype(BF16), n_heads)
        for j in range(n_blk):
            ref[0, j] = tb[:, j * KB:(j + 1) * KB]

    store_q(qa_ref, proj_t(R_QA, R_KA), N_HEADS_A, gain(0), True)
    store_k(ka_ref, proj_t(R_KA, R_VA), N_HEADS_A, gain(1), True)
    store_v(va_ref, proj_t(R_VA, R_QI), N_HEADS_A)

    t = proj_t(R_QI, R_WI + WI_ROWS)
    for i in range(IDX_HEADS):
        y = _rope_t(head(t, i), cos, sin) * (IDX_DIM ** -0.5)
        qi_ref[0, i * PAIR:(i + 1) * PAIR, :] = _pair_slot(y, False).astype(BF16)
    ki = _rope_t(_head_norm_t(head(t, IDX_HEADS), gain(2)), cos, sin)
    ki_blk = _pair_slot(ki, False).T.astype(BF16)
    for j in range(n_blk):
        ki_ref[0, j] = ki_blk[j * KB:(j + 1) * KB]
    wi_ref[0] = t[R_WI - R_QI:R_WI - R_QI + WI_ROWS] * (IDX_HEADS ** -0.5)

    store_q(qb_ref, proj_t(R_QB, R_KB), N_HEADS_B, gain(3), False)
    store_k(kb_ref, proj_t(R_KB, R_VB), N_HEADS_B, gain(4), False)
    store_v(vb_ref, proj_t(R_VB, R_QM), N_HEADS_B)
    store_q(qm_ref, proj_t(R_QM, R_END), N_HEADS_M, gain(5), False)


def _proj_call(x, pos3, invf, gmix, wt, layer, gains):
    B, S, _ = x.shape
    nkb = S // KB
    n_blk = TM // KB

    def tok_t(rows):
        return pl.BlockSpec((1, rows, TM), lambda b, i: (b, 0, i))

    def full(shape):
        return pl.BlockSpec(shape, lambda b, i: (0,) * len(shape))

    def krow(width):
        return pl.BlockSpec((1, n_blk, KB, width), lambda b, i: (b, i, 0, 0))

    def vt(rows):
        return pl.BlockSpec((1, n_blk, rows, KB), lambda b, i: (b, i, 0, 0))

    out_shape = (
        jax.ShapeDtypeStruct((B, N_HEADS_A * PAIR, S), BF16),
        jax.ShapeDtypeStruct((B, nkb, KB, WIDTH_A), BF16),
        jax.ShapeDtypeStruct((B, nkb, N_HEADS_A * V_ROWS, KB), BF16),
        jax.ShapeDtypeStruct((B, IDX_HEADS * PAIR, S), BF16),
        jax.ShapeDtypeStruct((B, nkb, KB, PAIR), BF16),
        jax.ShapeDtypeStruct((B, WI_ROWS, S), F32),
        jax.ShapeDtypeStruct((B, N_HEADS_B * PAIR, S), BF16),
        jax.ShapeDtypeStruct((B, nkb, KB, WIDTH_B), BF16),
        jax.ShapeDtypeStruct((B, nkb, N_HEADS_B * V_ROWS, KB), BF16),
        jax.ShapeDtypeStruct((B, N_HEADS_M * PAIR, S), BF16),
    )
    out_specs = (
        tok_t(N_HEADS_A * PAIR), krow(WIDTH_A), vt(N_HEADS_A * V_ROWS),
        tok_t(IDX_HEADS * PAIR), krow(PAIR), tok_t(WI_ROWS),
        tok_t(N_HEADS_B * PAIR), krow(WIDTH_B), vt(N_HEADS_B * V_ROWS),
        tok_t(N_HEADS_M * PAIR),
    )
    return pl.pallas_call(
        _proj_kernel,
        grid=(B, S // TM),
        in_specs=[
            pl.BlockSpec((1, TM, D_MODEL), lambda b, i: (b, i, 0)),
            pl.BlockSpec((1, 1, TM), lambda b, i: (b, 0, i)),
            full((ROT_HALF, 1)),
            full((1, D_MODEL)),
            pl.BlockSpec((None, R_END, D_MODEL), lambda b, i: (layer, 0, 0)),
            full((6, HEAD_DIM, 1)),
        ],
        out_specs=out_specs,
        out_shape=out_shape,
        compiler_params=pltpu.CompilerParams(
            dimension_semantics=("parallel", "parallel"),
            vmem_limit_bytes=VMEM_LIMIT),
        name="in_proj",
    )(x, pos3, invf, gmix, wt, gains)


def _memkv_kernel(mem_ref, gmem_ref, wt_ref, gk_ref, mk_ref, mv_ref):
    hm = _row_rms(mem_ref[0], gmem_ref[...]).astype(BF16)
    kt = _dot_nt(wt_ref[0:WIDTH_M, :], hm)
    for p in range(N_HEADS_M // 2):
        ys = [_head_norm_t(kt[i * HEAD_DIM:(i + 1) * HEAD_DIM], gk_ref[...])
              for i in (2 * p, 2 * p + 1)]
        mk_ref[0, :, p * PAIR:(p + 1) * PAIR] = (
            jnp.concatenate(ys, axis=0).T.astype(BF16))
    mv_ref[0] = _with_ones(
        _dot_nt(wt_ref[WIDTH_M:2 * WIDTH_M, :], hm).astype(BF16), N_HEADS_M)


def _memkv_call(mem, gmem, wt, layer, gk):
    B, n_mem, _ = mem.shape
    return pl.pallas_call(
        _memkv_kernel,
        grid=(B,),
        in_specs=[
            pl.BlockSpec((1, n_mem, D_MODEL), lambda b: (b, 0, 0)),
            pl.BlockSpec((1, D_MODEL), lambda b: (0, 0)),
            pl.BlockSpec((None, 2 * WIDTH_M, D_MODEL), lambda b: (layer, 0, 0)),
            pl.BlockSpec((HEAD_DIM, 1), lambda b: (0, 0)),
        ],
        out_specs=(
            pl.BlockSpec((1, n_mem, WIDTH_M), lambda b: (b, 0, 0)),
            pl.BlockSpec((1, N_HEADS_M * V_ROWS, n_mem), lambda b: (b, 0, 0)),
        ),
        out_shape=(
            jax.ShapeDtypeStruct((B, n_mem, WIDTH_M), BF16),
            jax.ShapeDtypeStruct((B, N_HEADS_M * V_ROWS, n_mem), BF16),
        ),
        compiler_params=pltpu.CompilerParams(
            dimension_semantics=("parallel",), vmem_limit_bytes=VMEM_LIMIT),
        name="mem_kv",
    )(mem, gmem, wt, gk)


def _bias_kernel(bnd_ref, f_ref, out_ref):
    n_keys = BAND_BLOCKS * KB
    width = f_ref.shape[-1]
    base = jnp.broadcast_to(f_ref[0, 0], (n_keys, width))
    rolled = pltpu.roll(base, width - (n_keys - 1), 1, stride=1, stride_axis=0)
    tab = rolled[:, 0:TQ]
    key_c = _chunk_of(lax.broadcasted_iota(I32, (n_keys, TQ), 0))
    qry_c = _chunk_of(lax.broadcasted_iota(I32, (n_keys, TQ), 1)
                      + (BAND_BLOCKS - 1) * KB)
    ok = (key_c <= qry_c) & (key_c >= qry_c - PREV_CHUNKS)
    shift = bnd_ref[pl.program_id(0), 1]
    out_ref[0, 0] = jnp.where(ok, tab * LOG2E - shift, NEG_INF)


def _bias_call(bounds, fvec):
    depth, n_heads, _, width = fvec.shape
    n_keys = BAND_BLOCKS * KB
    return pl.pallas_call(
        _bias_kernel,
        grid=(depth, n_heads),
        in_specs=[pl.BlockSpec(memory_space=pltpu.SMEM),
                  pl.BlockSpec((1, 1, 1, width), lambda l, h: (l, h, 0, 0))],
        out_specs=pl.BlockSpec((1, 1, n_keys, TQ), lambda l, h: (l, h, 0, 0)),
        out_shape=jax.ShapeDtypeStruct((depth, n_heads, n_keys, TQ), F32),
        compiler_params=pltpu.CompilerParams(
            dimension_semantics=("parallel", "parallel")),
        name="band_bias",
    )(bounds, fvec)


def _col_max(x):
    return jnp.max(x, axis=0, keepdims=True)


def _col_sum(x):
    return jnp.sum(x, axis=0, keepdims=True)


def _finish_pair(outs):
    return jnp.concatenate(outs, axis=0).T.astype(BF16)


def _dsa_kernel(bnd_ref, qa_ref, ka_ref, va_ref, qi_ref, ki_ref, wi_ref, out_ref,
                keys_ref, planes_ref, mask_ref, acc_ref, *, layer):
    t = pl.program_id(1)
    shift = bnd_ref[layer, 0]
    nkb = t + 1
    n_kb = keys_ref.shape[0]
    n_pair = nkb // 2
    odd = nkb % 2 == 1
    qry_c = _chunk_of(t * TQ + lax.broadcasted_iota(I32, (KB, TQ), 1))
    key_c0 = _chunk_of(lax.broadcasted_iota(I32, (KB, TQ), 0))

    def select_topk():
        def score_blocks(kbs):
            dss = [[_dot(ki_ref[0, kb], qi_ref[0, i * PAIR:(i + 1) * PAIR, :])
                    for i in range(IDX_HEADS)] for kb in kbs]
            for kb, ds in zip(kbs, dss):
                s = jnp.zeros((KB, TQ), F32)
                for i in range(IDX_HEADS):
                    s = s + jnp.maximum(ds[i], 0.0) * wi_ref[0, i:i + 1, :]
                s = jnp.where(s == 0.0, 0.0, s)
                bits = lax.bitcast_convert_type(s, I32)
                key = jnp.where(bits >= 0, bits, bits ^ jnp.int32(0x7FFFFFFF))
                allowed = (key_c0 + kb * (KB // CHUNK)) <= qry_c
                key = jnp.where(allowed, key, jnp.int32(INT_MIN))
                keys_ref[kb] = key
                u = key ^ jnp.int32(INT_MIN)
                cols = _bit_transpose32(
                    [u[j * SUBLANES:(j + 1) * SUBLANES] for j in range(KEY_BITS)])
                for b in range(KEY_BITS):
                    planes_ref[b, kb] = cols[KEY_BITS - 1 - b]

        def score_pair(j, carry):
            score_blocks([2 * j, 2 * j + 1])
            return carry

        lax.fori_loop(0, n_pair, score_pair, 0)

        @pl.when(odd)
        def _():
            score_blocks([nkb - 1])

        def col_count(words):
            return _col_sum(_tree_sum([lax.population_count(w) for w in words]))

        def radix_select(n_blocks):
            def bit(i, carry):
                active, rank, thr_u = carry
                b = KEY_BITS - 1 - i
                planes = planes_ref[b, 0:n_blocks]
                ones = [active[kb] & planes[kb] for kb in range(n_blocks)]
                c1 = col_count(ones)
                take = c1 >= rank
                active = tuple(jnp.where(take, ones[kb], active[kb] ^ ones[kb])
                               for kb in range(n_blocks))
                rank = jnp.where(take, rank, rank - c1)
                thr_u = jnp.where(
                    take, thr_u | lax.shift_left(jnp.int32(1), b), thr_u)
                return active, rank, thr_u

            active0 = tuple(
                jnp.full((SUBLANES, TQ), jnp.where(kb < nkb, -1, 0), I32)
                for kb in range(n_blocks))
            active, rank, thr_u = lax.fori_loop(
                0, KEY_BITS, bit,
                (active0, jnp.full((1, TQ), TOPK_MAX, I32), jnp.zeros((1, TQ), I32)))
            return rank, thr_u, col_count(active)

        rank, thr_u, n_eq = lax.cond(nkb <= n_kb // 2,
                                     lambda: radix_select(n_kb // 2),
                                     lambda: radix_select(n_kb))
        thr = thr_u ^ jnp.int32(INT_MIN)
        n_ge = TOPK_MAX - rank + n_eq
        tie = (n_ge > TOPK_MAX) & (thr > INT_MIN)
        any_tie = jnp.max(jnp.where(tie, 1.0, 0.0)) > 0.0

        @pl.when(jnp.logical_not(any_tie))
        def _():
            floor = jnp.maximum(thr, jnp.int32(INT_MIN + 1))

            def blk(kb, carry):
                mask_ref[kb] = jnp.where(keys_ref[kb] >= floor, -shift, MASKED)
                return carry

            lax.fori_loop(0, nkb, blk, 0)

        @pl.when(any_tie)
        def _():
            lower = (lax.broadcasted_iota(I32, (KB, KB), 1)
                     < lax.broadcasted_iota(I32, (KB, KB), 0))
            ltri = jnp.where(lower, 1.0, 0.0).astype(BF16)
            need = rank.astype(F32)

            def blk(kb, seen):
                k = keys_ref[kb]
                eq = jnp.where(k == thr, 1.0, 0.0)
                before = _dot(ltri, eq.astype(BF16)) + seen
                keep = (k > thr) | ((k == thr) & (before < need))
                keep = keep & (k > INT_MIN)
                mask_ref[kb] = jnp.where(keep, -shift, MASKED)
                return seen + _col_sum(eq)

            lax.fori_loop(0, nkb, blk, jnp.zeros((1, TQ), F32))

    @pl.when(nkb * KB <= TOPK_MAX)
    def _():
        mask_ref[0] = jnp.where(key_c0 <= qry_c, -shift, MASKED)
        planes_ref[:, 1:] = jnp.zeros_like(planes_ref[:, 1:])

    @pl.when(nkb * KB > TOPK_MAX)
    def _():
        select_topk()

    acc_ref[...] = jnp.zeros_like(acc_ref)
    heads = range(N_HEADS_A)

    def logits(kbs):
        masks = [mask_ref[kb] for kb in kbs]
        return [[_dot(ka_ref[0, kb, :, (h // 2) * PAIR:(h // 2 + 1) * PAIR],
                      qa_ref[0, h * PAIR:(h + 1) * PAIR, :]) + mask
                 for kb, mask in zip(kbs, masks)] for h in heads]

    def weighted_values(h, kbs, es):
        return functools.reduce(jnp.add, [
            _dot(va_ref[0, kb, h * V_ROWS:(h + 1) * V_ROWS, :], e.astype(BF16))
            for kb, e in zip(kbs, es)])

    @pl.when(shift <= SHIFT_LIMIT)
    def _():
        def blocks(kbs):
            lgs = logits(kbs)
            for h in heads:
                acc_ref[h] += weighted_values(
                    h, kbs, [jnp.exp2(lg) for lg in lgs[h]])

        def pair(j, carry):
            blocks([2 * j, 2 * j + 1])
            return carry

        lax.fori_loop(0, n_pair, pair, 0)

        @pl.when(odd)
        def _():
            blocks([nkb - 1])

    @pl.when(shift > SHIFT_LIMIT)
    def _():
        def blocks(kbs, ms):
            lgs = logits(kbs)
            ms_new = []
            for h in heads:
                m_new = functools.reduce(
                    jnp.maximum, [ms[h]] + [_col_max(lg) for lg in lgs[h]])
                acc_ref[h] = (acc_ref[h] * jnp.exp2(ms[h] - m_new)
                              + weighted_values(
                                  h, kbs, [jnp.exp2(lg - m_new) for lg in lgs[h]]))
                ms_new.append(m_new)
            return tuple(ms_new)

        def pair(j, ms):
            return blocks([2 * j, 2 * j + 1], ms)

        m0 = tuple(jnp.full((1, TQ), MASKED, F32) for _ in heads)
        ms = lax.fori_loop(0, n_pair, pair, m0)

        @pl.when(odd)
        def _():
            blocks([nkb - 1], ms)

    def head_out(h):
        return acc_ref[h, 0:HEAD_DIM] / acc_ref[h, HEAD_DIM:HEAD_DIM + 1]

    for p in range(N_HEADS_A // 2):
        outs = [head_out(h) for h in (2 * p, 2 * p + 1)]
        out_ref[0, :, p * PAIR:(p + 1) * PAIR] = _finish_pair(outs)


def _dsa_call(bounds, layer, qa, ka, va, qi, ki, wi):
    B, nkb = ka.shape[0], ka.shape[1]
    S = nkb * KB

    def tile_t(rows):
        return pl.BlockSpec((1, rows, TQ), lambda b, t: (b, 0, t))

    def whole(a):
        return pl.BlockSpec((1,) + a.shape[1:], lambda b, t: (b, 0, 0, 0))

    return pl.pallas_call(
        functools.partial(_dsa_kernel, layer=layer),
        grid=(B, S // TQ),
        in_specs=[pl.BlockSpec(memory_space=pltpu.SMEM), tile_t(N_HEADS_A * PAIR), whole(ka), whole(va),
                  tile_t(IDX_HEADS * PAIR), whole(ki), tile_t(WI_ROWS)],
        out_specs=pl.BlockSpec((1, TQ, WIDTH_A), lambda b, t: (b, t, 0)),
        out_shape=jax.ShapeDtypeStruct((B, S, WIDTH_A), BF16),
        scratch_shapes=[
            pltpu.VMEM((nkb, KB, TQ), I32),
            pltpu.VMEM((KEY_BITS, nkb, SUBLANES, TQ), I32),
            pltpu.VMEM((nkb, KB, TQ), F32),
            pltpu.VMEM((N_HEADS_A, V_ROWS, TQ), F32),
        ],
        compiler_params=pltpu.CompilerParams(
            dimension_semantics=("parallel", "arbitrary"),
            vmem_limit_bytes=VMEM_LIMIT),
        name="dsa_attn",
    )(bounds, qa, ka, va, qi, ki, wi)


def _band_mem_kernel(bnd_ref, qb_ref, kb_ref, vb_ref, bias_ref, qm_ref, mk_ref,
                     mv_ref, out_ref, *, layer):
    t = pl.program_id(1)
    shift_m = bnd_ref[layer, 2]
    small = jnp.maximum(bnd_ref[layer, 1], shift_m) <= SHIFT_LIMIT
    srcs, pens = [], []
    for r in range(BAND_BLOCKS):
        blk = t - (BAND_BLOCKS - 1) + r
        srcs.append(jnp.maximum(blk, 0))
        pens.append(jnp.where(blk >= 0, 0.0, NEG_INF))

    def v_rows(h):
        return slice(h * V_ROWS, (h + 1) * V_ROWS)

    def normalised(pv):
        return pv[0:HEAD_DIM] / pv[HEAD_DIM:HEAD_DIM + 1]

    def pair_cols(h):
        return slice((h // 2) * PAIR, (h // 2 + 1) * PAIR)

    def attend(use_max):
        band_lg = [[_dot(kb_ref[0, srcs[r], :, pair_cols(h)],
                         qb_ref[0, h * PAIR:(h + 1) * PAIR, :])
                    + bias_ref[0, h, r * KB:(r + 1) * KB, :] + pens[r]
                    for r in range(BAND_BLOCKS)] for h in range(N_HEADS_B)]
        mem_lg = [_dot(mk_ref[0, :, pair_cols(h)],
                       qm_ref[0, h * PAIR:(h + 1) * PAIR, :]) - shift_m
                  for h in range(N_HEADS_M)]
        band_e = []
        for lgs in band_lg:
            if use_max:
                m = functools.reduce(jnp.maximum, [_col_max(lg) for lg in lgs])
                lgs = [lg - m for lg in lgs]
            band_e.append([jnp.exp2(lg).astype(BF16) for lg in lgs])
        mem_e = [jnp.exp2(lg - _col_max(lg) if use_max else lg).astype(BF16)
                 for lg in mem_lg]
        band_o = [normalised(functools.reduce(jnp.add, [
            _dot(vb_ref[0, srcs[r], v_rows(h), :], band_e[h][r])
            for r in range(BAND_BLOCKS)])) for h in range(N_HEADS_B)]
        mem_o = [normalised(_dot(mv_ref[0, v_rows(h), :], mem_e[h]))
                 for h in range(N_HEADS_M)]
        for p in range(N_HEADS_B // 2):
            out_ref[0, :, p * PAIR:(p + 1) * PAIR] = _finish_pair(
                band_o[2 * p:2 * p + 2])
        for p in range(N_HEADS_M // 2):
            out_ref[0, :, WIDTH_B + p * PAIR:WIDTH_B + (p + 1) * PAIR] = (
                _finish_pair(mem_o[2 * p:2 * p + 2]))

    @pl.when(small)
    def _():
        attend(False)

    @pl.when(jnp.logical_not(small))
    def _():
        attend(True)


def _band_mem_call(bounds, qb, kb, vb, bias, layer, qm, mk, mv):
    B, nkb = kb.shape[0], kb.shape[1]
    S = nkb * KB
    n_mem = mk.shape[1]

    def tile_t(rows):
        return pl.BlockSpec((1, rows, TQ), lambda b, t: (b, 0, t))

    def whole(a):
        return pl.BlockSpec((1,) + a.shape[1:], lambda b, t: (b, 0, 0, 0))

    return pl.pallas_call(
        functools.partial(_band_mem_kernel, layer=layer),
        grid=(B, S // TQ),
        in_specs=[
            pl.BlockSpec(memory_space=pltpu.SMEM),
            tile_t(N_HEADS_B * PAIR), whole(kb), whole(vb),
            pl.BlockSpec((1,) + bias.shape[1:], lambda b, t: (layer, 0, 0, 0)),
            tile_t(N_HEADS_M * PAIR),
            pl.BlockSpec((1, n_mem, WIDTH_M), lambda b, t: (b, 0, 0)),
            pl.BlockSpec((1, N_HEADS_M * V_ROWS, n_mem), lambda b, t: (b, 0, 0)),
        ],
        out_specs=pl.BlockSpec((1, TQ, WIDTH_B + WIDTH_M), lambda b, t: (b, t, 0)),
        out_shape=jax.ShapeDtypeStruct((B, S, WIDTH_B + WIDTH_M), BF16),
        compiler_params=pltpu.CompilerParams(
            dimension_semantics=("parallel", "parallel"),
            vmem_limit_bytes=VMEM_LIMIT),
        name="band_mem_attn",
    )(bounds, qb, kb, vb, bias, qm, mk, mv)


FF_CHUNK = 256


def _out_ffn_kernel(x_ref, oa_ref, obm_ref, wo_ref, gffn_ref, wgu_ref, wd_ref,
                    out_ref):
    attn = jnp.concatenate([oa_ref[...], obm_ref[...]], axis=1)
    x1 = x_ref[...] + _dot(attn, wo_ref[...])
    h = _row_rms(x1, gffn_ref[...]).astype(BF16)
    acc = x1
    for c in range(D_FF // FF_CHUNK):
        c0 = c * FF_CHUNK
        gate = _dot(h, wgu_ref[:, c0:c0 + FF_CHUNK])
        up = _dot(h, wgu_ref[:, D_FF + c0:D_FF + c0 + FF_CHUNK])
        act = gate * (1.0 / (1.0 + jnp.exp(-gate))) * up
        acc = acc + _dot(act.astype(BF16), wd_ref[c0:c0 + FF_CHUNK, :])
    out_ref[...] = acc


def _out_ffn_call(x2, oa2, obm2, wo, gffn, wgu, wd, layer):
    n = x2.shape[0]

    def rows(width):
        return pl.BlockSpec((TF, width), lambda i: (i, 0))

    def resident(a):
        return pl.BlockSpec((None,) + a.shape[1:], lambda i: (layer, 0, 0),
                            pipeline_mode=pl.Buffered(1))

    return pl.pallas_call(
        _out_ffn_kernel,
        grid=(n // TF,),
        in_specs=[rows(D_MODEL), rows(WIDTH_A), rows(WIDTH_B + WIDTH_M),
                  resident(wo), resident(gffn), resident(wgu), resident(wd)],
        out_specs=rows(D_MODEL),
        out_shape=jax.ShapeDtypeStruct((n, D_MODEL), F32),
        compiler_params=pltpu.CompilerParams(
            dimension_semantics=("parallel",), vmem_limit_bytes=VMEM_LIMIT),
        name="out_ffn",
    )(x2, oa2, obm2, wo, gffn, wgu, wd)


def kernel(x, mem, positions, g_mix, w_in, g_q_a, g_k_a, g_k_idx, g_q_b, g_k_b,
           rel_bias, g_q_m, g_k_m, g_mem, w_mem_kv, w_out, g_ffn, w_gate_up,
           w_down):
    B, S, D = x.shape
    depth = w_in.shape[0]

    w_in_t = jnp.swapaxes(w_in.astype(BF16), 1, 2)
    w_mem_t = jnp.swapaxes(w_mem_kv.astype(BF16), 1, 2)
    w_out_b = w_out.astype(BF16)
    w_gu_b = w_gate_up.astype(BF16)
    w_down_b = w_down.astype(BF16)
    gains = jnp.stack(
        [g_q_a, g_k_a, g_k_idx, g_q_b, g_k_b, g_q_m], axis=1)[..., None]

    inv_freq = jnp.power(
        ROPE_THETA, -jnp.arange(ROT_HALF, dtype=F32) / ROT_HALF)[:, None]
    pos3 = positions.reshape(B, 1, S)

    n_f = 2 * REL_CLIP
    fvec = jnp.concatenate(
        [rel_bias[..., 1:], jnp.broadcast_to(rel_bias[..., -1:],
                                             rel_bias.shape[:-1] + (n_f,))],
        axis=-1)[:, :, None, :]
    def amax(g):
        return jnp.max(jnp.abs(g), axis=-1)

    c = HEAD_DIM ** 0.5 * LOG2E * SHIFT_MARGIN
    bounds = jnp.stack(
        [c * amax(g_q_a) * amax(g_k_a),
         c * amax(g_q_b) * amax(g_k_b) + LOG2E * jnp.max(jnp.abs(rel_bias), (1, 2)),
         c * amax(g_q_m) * amax(g_k_m),
         jnp.zeros((depth,), F32)], axis=1)
    bias = _bias_call(bounds, fvec)

    for l in range(depth):
        qa, ka, va, qi, ki, wi, qb, kb, vb, qm = _proj_call(
            x, pos3, inv_freq, g_mix[l][None], w_in_t, l, gains[l])
        mk, mv = _memkv_call(mem, g_mem[l][None], w_mem_t, l, g_k_m[l][:, None])
        oa = _dsa_call(bounds, l, qa, ka, va, qi, ki, wi)
        obm = _band_mem_call(bounds, qb, kb, vb, bias, l, qm, mk, mv)
        x = _out_ffn_call(
            x.reshape(B * S, D), oa.reshape(B * S, WIDTH_A),
            obm.reshape(B * S, WIDTH_B + WIDTH_M), w_out_b, g_ffn[:, None],
            w_gu_b, w_down_b, l).reshape(B, S, D)
    return x
```

```python
import functools

import jax
import jax.numpy as jnp
from jax import lax
from jax.experimental import pallas as pl
from jax.experimental.pallas import tpu as pltpu

F32 = jnp.float32
BF16 = jnp.bfloat16
I32 = jnp.int32

D_MODEL = 1024
HEAD_DIM = 64
N_HEADS_A = 6
N_HEADS_B = 6
N_HEADS_M = 4
WIDTH_A = N_HEADS_A * HEAD_DIM
WIDTH_B = N_HEADS_B * HEAD_DIM
WIDTH_M = N_HEADS_M * HEAD_DIM
IDX_HEADS = 4
IDX_DIM = 64
ROT_DIM = HEAD_DIM // 4
ROT_HALF = ROT_DIM // 2
ROPE_THETA = 500000.0
CHUNK = 64
PREV_CHUNKS = 8
REL_CLIP = 256
TOPK_MAX = 256
D_FF = 2816
EPS = 1e-6
LOG2E = 1.4426950408889634

LANES = 128
SUBLANES = 8
PAIR = 2 * HEAD_DIM
V_ROWS = HEAD_DIM + 16
TM = 1024
TF = 512
TQ = 256
KB = 256
BAND_BLOCKS = 3
WI_ROWS = 8
INT_MIN = -(2 ** 31)
KEY_BITS = 32
NEG_INF = float("-inf")
MASKED = -1e30
SHIFT_LIMIT = 40.0
SHIFT_MARGIN = 1.02
VMEM_LIMIT = 56 * 1024 * 1024

R_QA = 0
R_KA = R_QA + WIDTH_A
R_VA = R_KA + WIDTH_A
R_QI = R_VA + WIDTH_A
R_KI = R_QI + IDX_HEADS * IDX_DIM
R_WI = R_KI + IDX_DIM
R_QB = R_WI + IDX_HEADS
R_KB = R_QB + WIDTH_B
R_VB = R_KB + WIDTH_B
R_QM = R_VB + WIDTH_B
R_END = R_QM + WIDTH_M

NT_DIMS = (((1,), (1,)), ((), ()))


def _dot(a, b):
    return jnp.dot(a, b, preferred_element_type=F32)


def _dot_nt(a, b):
    return lax.dot_general(a, b, NT_DIMS, preferred_element_type=F32)


def _chunk_of(pos):
    return jnp.right_shift(pos, CHUNK.bit_length() - 1)


def _tree_sum(xs):
    while len(xs) > 1:
        xs = [a + b for a, b in zip(xs[0::2], xs[1::2])] + xs[len(xs) & ~1:]
    return xs[0]


def _as_i32(v):
    return v - (1 << 32) if v >= (1 << 31) else v


def _bit_transpose32(rows):
    rows = list(rows)
    j, m = 16, 0x0000FFFF
    while j:
        k = 0
        while k < 32:
            t = ((rows[k] ^ lax.shift_right_logical(rows[k + j], jnp.int32(j)))
                 & jnp.int32(_as_i32(m)))
            rows[k] = rows[k] ^ t
            rows[k + j] = rows[k + j] ^ lax.shift_left(t, jnp.int32(j))
            k = (k + j + 1) & ~j
        j >>= 1
        if j:
            m = (m ^ (m << j)) & 0xFFFFFFFF
    return rows


def _row_rms(x, g):
    ms = jnp.mean(x * x, axis=-1, keepdims=True)
    return x * lax.rsqrt(ms + EPS) * g


def _head_norm_t(blk, g_col):
    ms = jnp.mean(blk * blk, axis=0, keepdims=True)
    return blk * lax.rsqrt(ms + EPS) * g_col


def _rope_t(y, cos, sin):
    y1 = y[0:ROT_HALF]
    y2 = y[ROT_HALF:ROT_DIM]
    return jnp.concatenate(
        [y1 * cos - y2 * sin, y2 * cos + y1 * sin, y[ROT_DIM:]], axis=0)


def _with_ones(v_heads, n_heads):
    ones = jnp.ones((V_ROWS - HEAD_DIM, v_heads.shape[1]), v_heads.dtype)
    parts = []
    for i in range(n_heads):
        parts += [v_heads[i * HEAD_DIM:(i + 1) * HEAD_DIM], ones]
    return jnp.concatenate(parts, axis=0)


def _pair_slot(y, odd):
    z = jnp.zeros_like(y)
    return jnp.concatenate([z, y] if odd else [y, z], axis=0)


def _proj_kernel(x_ref, pos_ref, invf_ref, gmix_ref, wt_ref, gains_ref,
                 qa_ref, ka_ref, va_ref, qi_ref, ki_ref, wi_ref,
                 qb_ref, kb_ref, vb_ref, qm_ref):
    h = _row_rms(x_ref[0], gmix_ref[...]).astype(BF16)
    ang = invf_ref[...] * pos_ref[0].astype(F32)
    cos = jnp.cos(ang)
    sin = jnp.sin(ang)
    q_scale = HEAD_DIM ** -0.5 * LOG2E
    n_blk = TM // KB

    def proj_t(r0, r1):
        return _dot_nt(wt_ref[r0:r1, :], h)

    def gain(i):
        return gains_ref[i]

    def head(t, i):
        return t[i * HEAD_DIM:(i + 1) * HEAD_DIM]

    def store_q(ref, t, n_heads, g, rope):
        for i in range(n_heads):
            y = head(t, i)
            if g is not None:
                y = _head_norm_t(y, g)
            if rope:
                y = _rope_t(y, cos, sin)
            ref[0, i * PAIR:(i + 1) * PAIR, :] = _pair_slot(
                y * q_scale, i % 2 == 1).astype(BF16)

    def store_k(ref, t, n_heads, g, rope):
        for p in range(n_heads // 2):
            ys = []
            for i in (2 * p, 2 * p + 1):
                y = _head_norm_t(head(t, i), g)
                ys.append(_rope_t(y, cos, sin) if rope else y)
            blk = jnp.concatenate(ys, axis=0).T.astype(BF16)
            for j in range(n_blk):
                ref[0, j, :, p * PAIR:(p + 1) * PAIR] = blk[j * KB:(j + 1) * KB]

    def store_v(ref, t, n_heads):
        tb = _with_ones(t.astype(BF16), n_heads)
        for j in range(n_blk):
            ref[0, j] = tb[:, j * KB:(j + 1) * KB]

    store_q(qa_ref, proj_t(R_QA, R_KA), N_HEADS_A, gain(0), True)
    store_k(ka_ref, proj_t(R_KA, R_VA), N_HEADS_A, gain(1), True)
    store_v(va_ref, proj_t(R_VA, R_QI), N_HEADS_A)

    t = proj_t(R_QI, R_WI + WI_ROWS)
    for i in range(IDX_HEADS):
        y = _rope_t(head(t, i), cos, sin) * (IDX_DIM ** -0.5)
        qi_ref[0, i * PAIR:(i + 1) * PAIR, :] = _pair_slot(y, False).astype(BF16)
    ki = _rope_t(_head_norm_t(head(t, IDX_HEADS), gain(2)), cos, sin)
    ki_blk = _pair_slot(ki, False).T.astype(BF16)
    for j in range(n_blk):
        ki_ref[0, j] = ki_blk[j * KB:(j + 1) * KB]
    wi_ref[0] = t[R_WI - R_QI:R_WI - R_QI + WI_ROWS] * (IDX_HEADS ** -0.5)

    store_q(qb_ref, proj_t(R_QB, R_KB), N_HEADS_B, gain(3), False)
    store_k(kb_ref, proj_t(R_KB, R_VB), N_HEADS_B, gain(4), False)
    store_v(vb_ref, proj_t(R_VB, R_QM), N_HEADS_B)
    store_q(qm_ref, proj_t(R_QM, R_END), N_HEADS_M, gain(5), False)


def _proj_call(x, pos3, invf, gmix, wt, layer, gains):
    B, S, _ = x.shape
    nkb = S // KB
    n_blk = TM // KB

    def tok_t(rows):
        return pl.BlockSpec((1, rows, TM), lambda b, i: (b, 0, i))

    def full(shape):
        return pl.BlockSpec(shape, lambda b, i: (0,) * len(shape))

    def krow(width):
        return pl.BlockSpec((1, n_blk, KB, width), lambda b, i: (b, i, 0, 0))

    def vt(rows):
        return pl.BlockSpec((1, n_blk, rows, KB), lambda b, i: (b, i, 0, 0))

    out_shape = (
        jax.ShapeDtypeStruct((B, N_HEADS_A * PAIR, S), BF16),
        jax.ShapeDtypeStruct((B, nkb, KB, WIDTH_A), BF16),
        jax.ShapeDtypeStruct((B, nkb, N_HEADS_A * V_ROWS, KB), BF16),
        jax.ShapeDtypeStruct((B, IDX_HEADS * PAIR, S), BF16),
        jax.ShapeDtypeStruct((B, nkb, KB, PAIR), BF16),
        jax.ShapeDtypeStruct((B, WI_ROWS, S), F32),
        jax.ShapeDtypeStruct((B, N_HEADS_B * PAIR, S), BF16),
        jax.ShapeDtypeStruct((B, nkb, KB, WIDTH_B), BF16),
        jax.ShapeDtypeStruct((B, nkb, N_HEADS_B * V_ROWS, KB), BF16),
        jax.ShapeDtypeStruct((B, N_HEADS_M * PAIR, S), BF16),
    )
    out_specs = (
        tok_t(N_HEADS_A * PAIR), krow(WIDTH_A), vt(N_HEADS_A * V_ROWS),
        tok_t(IDX_HEADS * PAIR), krow(PAIR), tok_t(WI_ROWS),
        tok_t(N_HEADS_B * PAIR), krow(WIDTH_B), vt(N_HEADS_B * V_ROWS),
        tok_t(N_HEADS_M * PAIR),
    )
    return pl.pallas_call(
        _proj_kernel,
        grid=(B, S // TM),
        in_specs=[
            pl.BlockSpec((1, TM, D_MODEL), lambda b, i: (b, i, 0)),
            pl.BlockSpec((1, 1, TM), lambda b, i: (b, 0, i)),
            full((ROT_HALF, 1)),
            full((1, D_MODEL)),
            pl.BlockSpec((None, R_END, D_MODEL), lambda b, i: (layer, 0, 0)),
            full((6, HEAD_DIM, 1)),
        ],
        out_specs=out_specs,
        out_shape=out_shape,
        compiler_params=pltpu.CompilerParams(
            dimension_semantics=("parallel", "parallel"),
            vmem_limit_bytes=VMEM_LIMIT),
        name="in_proj",
    )(x, pos3, invf, gmix, wt, gains)


def _memkv_kernel(mem_ref, gmem_ref, wt_ref, gk_ref, mk_ref, mv_ref):
    hm = _row_rms(mem_ref[0], gmem_ref[...]).astype(BF16)
    kt = _dot_nt(wt_ref[0:WIDTH_M, :], hm)
    for p in range(N_HEADS_M // 2):
        ys = [_head_norm_t(kt[i * HEAD_DIM:(i + 1) * HEAD_DIM], gk_ref[...])
              for i in (2 * p, 2 * p + 1)]
        mk_ref[0, :, p * PAIR:(p + 1) * PAIR] = (
            jnp.concatenate(ys, axis=0).T.astype(BF16))
    mv_ref[0] = _with_ones(
        _dot_nt(wt_ref[WIDTH_M:2 * WIDTH_M, :], hm).astype(BF16), N_HEADS_M)


def _memkv_call(mem, gmem, wt, layer, gk):
    B, n_mem, _ = mem.shape
    return pl.pallas_call(
        _memkv_kernel,
        grid=(B,),
        in_specs=[
            pl.BlockSpec((1, n_mem, D_MODEL), lambda b: (b, 0, 0)),
            pl.BlockSpec((1, D_MODEL), lambda b: (0, 0)),
            pl.BlockSpec((None, 2 * WIDTH_M, D_MODEL), lambda b: (layer, 0, 0)),
            pl.BlockSpec((HEAD_DIM, 1), lambda b: (0, 0)),
        ],
        out_specs=(
            pl.BlockSpec((1, n_mem, WIDTH_M), lambda b: (b, 0, 0)),
            pl.BlockSpec((1, N_HEADS_M * V_ROWS, n_mem), lambda b: (b, 0, 0)),
        ),
        out_shape=(
            jax.ShapeDtypeStruct((B, n_mem, WIDTH_M), BF16),
            jax.ShapeDtypeStruct((B, N_HEADS_M * V_ROWS, n_mem), BF16),
        ),
        compiler_params=pltpu.CompilerParams(
            dimension_semantics=("parallel",), vmem_limit_bytes=VMEM_LIMIT),
        name="mem_kv",
    )(mem, gmem, wt, gk)


def _bias_kernel(bnd_ref, f_ref, out_ref):
    n_keys = BAND_BLOCKS * KB
    width = f_ref.shape[-1]
    base = jnp.broadcast_to(f_ref[0, 0], (n_keys, width))
    rolled = pltpu.roll(base, width - (n_keys - 1), 1, stride=1, stride_axis=0)
    tab = rolled[:, 0:TQ]
    key_c = _chunk_of(lax.broadcasted_iota(I32, (n_keys, TQ), 0))
    qry_c = _chunk_of(lax.broadcasted_iota(I32, (n_keys, TQ), 1)
                      + (BAND_BLOCKS - 1) * KB)
    ok = (key_c <= qry_c) & (key_c >= qry_c - PREV_CHUNKS)
    shift = bnd_ref[pl.program_id(0), 1]
    out_ref[0, 0] = jnp.where(ok, tab * LOG2E - shift, NEG_INF)


def _bias_call(bounds, fvec):
    depth, n_heads, _, width = fvec.shape
    n_keys = BAND_BLOCKS * KB
    return pl.pallas_call(
        _bias_kernel,
        grid=(depth, n_heads),
        in_specs=[pl.BlockSpec(memory_space=pltpu.SMEM),
                  pl.BlockSpec((1, 1, 1, width), lambda l, h: (l, h, 0, 0))],
        out_specs=pl.BlockSpec((1, 1, n_keys, TQ), lambda l, h: (l, h, 0, 0)),
        out_shape=jax.ShapeDtypeStruct((depth, n_heads, n_keys, TQ), F32),
        compiler_params=pltpu.CompilerParams(
            dimension_semantics=("parallel", "parallel")),
        name="band_bias",
    )(bounds, fvec)


def _col_max(x):
    return jnp.max(x, axis=0, keepdims=True)


def _col_sum(x):
    return jnp.sum(x, axis=0, keepdims=True)


def _finish_pair(outs):
    return jnp.concatenate(outs, axis=0).T.astype(BF16)


def _dsa_kernel(bnd_ref, qa_ref, ka_ref, va_ref, qi_ref, ki_ref, wi_ref, out_ref,
                keys_ref, planes_ref, mask_ref, acc_ref, *, layer):
    t = pl.program_id(1)
    shift = bnd_ref[layer, 0]
    nkb = t + 1
    n_kb = keys_ref.shape[0]
    n_pair = nkb // 2
    odd = nkb % 2 == 1
    qry_c = _chunk_of(t * TQ + lax.broadcasted_iota(I32, (KB, TQ), 1))
    key_c0 = _chunk_of(lax.broadcasted_iota(I32, (KB, TQ), 0))

    def select_topk():
        def score_blocks(kbs):
            dss = [[_dot(ki_ref[0, kb], qi_ref[0, i * PAIR:(i + 1) * PAIR, :])
                    for i in range(IDX_HEADS)] for kb in kbs]
            for kb, ds in zip(kbs, dss):
                s = jnp.zeros((KB, TQ), F32)
                for i in range(IDX_HEADS):
                    s = s + jnp.maximum(ds[i], 0.0) * wi_ref[0, i:i + 1, :]
                s = jnp.where(s == 0.0, 0.0, s)
                bits = lax.bitcast_convert_type(s, I32)
                key = jnp.where(bits >= 0, bits, bits ^ jnp.int32(0x7FFFFFFF))
                allowed = (key_c0 + kb * (KB // CHUNK)) <= qry_c
                key = jnp.where(allowed, key, jnp.int32(INT_MIN))
                keys_ref[kb] = key
                u = key ^ jnp.int32(INT_MIN)
                cols = _bit_transpose32(
                    [u[j * SUBLANES:(j + 1) * SUBLANES] for j in range(KEY_BITS)])
                for b in range(KEY_BITS):
                    planes_ref[b, kb] = cols[KEY_BITS - 1 - b]

        def score_pair(j, carry):
            score_blocks([2 * j, 2 * j + 1])
            return carry

        lax.fori_loop(0, n_pair, score_pair, 0)

        @pl.when(odd)
        def _():
            score_blocks([nkb - 1])

        def col_count(words):
            return _col_sum(_tree_sum([lax.population_count(w) for w in words]))

        def radix_select(n_blocks):
            def bit(i, carry):
                active, rank, thr_u = carry
                b = KEY_BITS - 1 - i
                planes = planes_ref[b, 0:n_blocks]
                ones = [active[kb] & planes[kb] for kb in range(n_blocks)]
                c1 = col_count(ones)
                take = c1 >= rank
                active = tuple(jnp.where(take, ones[kb], active[kb] ^ ones[kb])
                               for kb in range(n_blocks))
                rank = jnp.where(take, rank, rank - c1)
                thr_u = jnp.where(
                    take, thr_u | lax.shift_left(jnp.int32(1), b), thr_u)
                return active, rank, thr_u

            active0 = tuple(
                jnp.full((SUBLANES, TQ), jnp.where(kb < nkb, -1, 0), I32)
                for kb in range(n_blocks))
            active, rank, thr_u = lax.fori_loop(
                0, KEY_BITS, bit,
                (active0, jnp.full((1, TQ), TOPK_MAX, I32), jnp.zeros((1, TQ), I32)))
            return rank, thr_u, col_count(active)

        rank, thr_u, n_eq = lax.cond(nkb <= n_kb // 2,
                                     lambda: radix_select(n_kb // 2),
                                     lambda: radix_select(n_kb))
        thr = thr_u ^ jnp.int32(INT_MIN)
        n_ge = TOPK_MAX - rank + n_eq
        tie = (n_ge > TOPK_MAX) & (thr > INT_MIN)
        any_tie = jnp.max(jnp.where(tie, 1.0, 0.0)) > 0.0

        floor = jnp.maximum(thr, jnp.int32(INT_MIN + 1))

        def plain_blk(kb, carry):
            mask_ref[kb] = jnp.where(keys_ref[kb] >= floor, -shift, MASKED)
            return carry

        lax.fori_loop(0, nkb, plain_blk, 0)

        @pl.when(any_tie)
        def _():
            lower = (lax.broadcasted_iota(I32, (KB, KB), 1)
                     < lax.broadcasted_iota(I32, (KB, KB), 0))
            ltri = jnp.where(lower, 1.0, 0.0).astype(BF16)
            need = rank.astype(F32)

            def blk(kb, seen):
                k = keys_ref[kb]
                eq = jnp.where(k == thr, 1.0, 0.0)
                before = _dot(ltri, eq.astype(BF16)) + seen
                keep = (k > thr) | ((k == thr) & (before < need))
                keep = keep & (k > INT_MIN)
                mask_ref[kb] = jnp.where(keep, -shift, MASKED)
                return seen + _col_sum(eq)

            lax.fori_loop(0, nkb, blk, jnp.zeros((1, TQ), F32))

    @pl.when(nkb * KB <= TOPK_MAX)
    def _():
        mask_ref[0] = jnp.where(key_c0 <= qry_c, -shift, MASKED)
        planes_ref[:, 1:] = jnp.zeros_like(planes_ref[:, 1:])

    @pl.when(nkb * KB > TOPK_MAX)
    def _():
        select_topk()

    acc_ref[...] = jnp.zeros_like(acc_ref)
    heads = range(N_HEADS_A)

    def logits(kbs):
        masks = [mask_ref[kb] for kb in kbs]
        return [[_dot(ka_ref[0, kb, :, (h // 2) * PAIR:(h // 2 + 1) * PAIR],
                      qa_ref[0, h * PAIR:(h + 1) * PAIR, :]) + mask
                 for kb, mask in zip(kbs, masks)] for h in heads]

    def weighted_values(h, kbs, es):
        return functools.reduce(jnp.add, [
            _dot(va_ref[0, kb, h * V_ROWS:(h + 1) * V_ROWS, :], e.astype(BF16))
            for kb, e in zip(kbs, es)])

    @pl.when(shift <= SHIFT_LIMIT)
    def _():
        def blocks(kbs):
            lgs = logits(kbs)
            for h in heads:
                acc_ref[h] += weighted_values(
                    h, kbs, [jnp.exp2(lg) for lg in lgs[h]])

        def pair(j, carry):
            blocks([2 * j, 2 * j + 1])
            return carry

        lax.fori_loop(0, n_pair, pair, 0)

        @pl.when(odd)
        def _():
            blocks([nkb - 1])

    @pl.when(shift > SHIFT_LIMIT)
    def _():
        def blocks(kbs, ms):
            lgs = logits(kbs)
            ms_new = []
            for h in heads:
                m_new = functools.reduce(
                    jnp.maximum, [ms[h]] + [_col_max(lg) for lg in lgs[h]])
                acc_ref[h] = (acc_ref[h] * jnp.exp2(ms[h] - m_new)
                              + weighted_values(
                                  h, kbs, [jnp.exp2(lg - m_new) for lg in lgs[h]]))
                ms_new.append(m_new)
            return tuple(ms_new)

        def pair(j, ms):
            return blocks([2 * j, 2 * j + 1], ms)

        m0 = tuple(jnp.full((1, TQ), MASKED, F32) for _ in heads)
        ms = lax.fori_loop(0, n_pair, pair, m0)

        @pl.when(odd)
        def _():
            blocks([nkb - 1], ms)

    def head_out(h):
        return acc_ref[h, 0:HEAD_DIM] / acc_ref[h, HEAD_DIM:HEAD_DIM + 1]

    for p in range(N_HEADS_A // 2):
        outs = [head_out(h) for h in (2 * p, 2 * p + 1)]
        out_ref[0, :, p * PAIR:(p + 1) * PAIR] = _finish_pair(outs)


def _dsa_call(bounds, layer, qa, ka, va, qi, ki, wi):
    B, nkb = ka.shape[0], ka.shape[1]
    S = nkb * KB

    def tile_t(rows):
        return pl.BlockSpec((1, rows, TQ), lambda b, t: (b, 0, t))

    def whole(a):
        return pl.BlockSpec((1,) + a.shape[1:], lambda b, t: (b, 0, 0, 0))

    return pl.pallas_call(
        functools.partial(_dsa_kernel, layer=layer),
        grid=(B, S // TQ),
        in_specs=[pl.BlockSpec(memory_space=pltpu.SMEM), tile_t(N_HEADS_A * PAIR), whole(ka), whole(va),
                  tile_t(IDX_HEADS * PAIR), whole(ki), tile_t(WI_ROWS)],
        out_specs=pl.BlockSpec((1, TQ, WIDTH_A), lambda b, t: (b, t, 0)),
        out_shape=jax.ShapeDtypeStruct((B, S, WIDTH_A), BF16),
        scratch_shapes=[
            pltpu.VMEM((nkb, KB, TQ), I32),
            pltpu.VMEM((KEY_BITS, nkb, SUBLANES, TQ), I32),
            pltpu.VMEM((nkb, KB, TQ), F32),
            pltpu.VMEM((N_HEADS_A, V_ROWS, TQ), F32),
        ],
        compiler_params=pltpu.CompilerParams(
            dimension_semantics=("parallel", "arbitrary"),
            vmem_limit_bytes=VMEM_LIMIT),
        name="dsa_attn",
    )(bounds, qa, ka, va, qi, ki, wi)


def _band_mem_kernel(bnd_ref, qb_ref, kb_ref, vb_ref, bias_ref, qm_ref, mk_ref,
                     mv_ref, out_ref, *, layer):
    t = pl.program_id(1)
    shift_m = bnd_ref[layer, 2]
    small = jnp.maximum(bnd_ref[layer, 1], shift_m) <= SHIFT_LIMIT
    srcs, pens = [], []
    for r in range(BAND_BLOCKS):
        blk = t - (BAND_BLOCKS - 1) + r
        srcs.append(jnp.maximum(blk, 0))
        pens.append(jnp.where(blk >= 0, 0.0, NEG_INF))

    def v_rows(h):
        return slice(h * V_ROWS, (h + 1) * V_ROWS)

    def normalised(pv):
        return pv[0:HEAD_DIM] / pv[HEAD_DIM:HEAD_DIM + 1]

    def pair_cols(h):
        return slice((h // 2) * PAIR, (h // 2 + 1) * PAIR)

    def attend(use_max):
        band_lg = [[_dot(kb_ref[0, srcs[r], :, pair_cols(h)],
                         qb_ref[0, h * PAIR:(h + 1) * PAIR, :])
                    + bias_ref[0, h, r * KB:(r + 1) * KB, :] + pens[r]
                    for r in range(BAND_BLOCKS)] for h in range(N_HEADS_B)]
        mem_lg = [_dot(mk_ref[0, :, pair_cols(h)],
                       qm_ref[0, h * PAIR:(h + 1) * PAIR, :]) - shift_m
                  for h in range(N_HEADS_M)]
        band_e = []
        for lgs in band_lg:
            if use_max:
                m = functools.reduce(jnp.maximum, [_col_max(lg) for lg in lgs])
                lgs = [lg - m for lg in lgs]
            band_e.append([jnp.exp2(lg).astype(BF16) for lg in lgs])
        mem_e = [jnp.exp2(lg - _col_max(lg) if use_max else lg).astype(BF16)
                 for lg in mem_lg]
        band_o = [normalised(functools.reduce(jnp.add, [
            _dot(vb_ref[0, srcs[r], v_rows(h), :], band_e[h][r])
            for r in range(BAND_BLOCKS)])) for h in range(N_HEADS_B)]
        mem_o = [normalised(_dot(mv_ref[0, v_rows(h), :], mem_e[h]))
                 for h in range(N_HEADS_M)]
        for p in range(N_HEADS_B // 2):
            out_ref[0, :, p * PAIR:(p + 1) * PAIR] = _finish_pair(
                band_o[2 * p:2 * p + 2])
        for p in range(N_HEADS_M // 2):
            out_ref[0, :, WIDTH_B + p * PAIR:WIDTH_B + (p + 1) * PAIR] = (
                _finish_pair(mem_o[2 * p:2 * p + 2]))

    @pl.when(small)
    def _():
        attend(False)

    @pl.when(jnp.logical_not(small))
    def _():
        attend(True)


def _band_mem_call(bounds, qb, kb, vb, bias, layer, qm, mk, mv):
    B, nkb = kb.shape[0], kb.shape[1]
    S = nkb * KB
    n_mem = mk.shape[1]

    def tile_t(rows):
        return pl.BlockSpec((1, rows, TQ), lambda b, t: (b, 0, t))

    def whole(a):
        return pl.BlockSpec((1,) + a.shape[1:], lambda b, t: (b, 0, 0, 0))

    return pl.pallas_call(
        functools.partial(_band_mem_kernel, layer=layer),
        grid=(B, S // TQ),
        in_specs=[
            pl.BlockSpec(memory_space=pltpu.SMEM),
            tile_t(N_HEADS_B * PAIR), whole(kb), whole(vb),
            pl.BlockSpec((1,) + bias.shape[1:], lambda b, t: (layer, 0, 0, 0)),
            tile_t(N_HEADS_M * PAIR),
            pl.BlockSpec((1, n_mem, WIDTH_M), lambda b, t: (b, 0, 0)),
            pl.BlockSpec((1, N_HEADS_M * V_ROWS, n_mem), lambda b, t: (b, 0, 0)),
        ],
        out_specs=pl.BlockSpec((1, TQ, WIDTH_B + WIDTH_M), lambda b, t: (b, t, 0)),
        out_shape=jax.ShapeDtypeStruct((B, S, WIDTH_B + WIDTH_M), BF16),
        compiler_params=pltpu.CompilerParams(
            dimension_semantics=("parallel", "parallel"),
            vmem_limit_bytes=VMEM_LIMIT),
        name="band_mem_attn",
    )(bounds, qb, kb, vb, bias, qm, mk, mv)


FF_CHUNK = 256


def _out_ffn_kernel(x_ref, oa_ref, obm_ref, wo_ref, gffn_ref, wgu_ref, wd_ref,
                    out_ref):
    attn = jnp.concatenate([oa_ref[...], obm_ref[...]], axis=1)
    x1 = x_ref[...] + _dot(attn, wo_ref[...])
    h = _row_rms(x1, gffn_ref[...]).astype(BF16)
    acc = x1
    for c in range(D_FF // FF_CHUNK):
        c0 = c * FF_CHUNK
        gate = _dot(h, wgu_ref[:, c0:c0 + FF_CHUNK])
        up = _dot(h, wgu_ref[:, D_FF + c0:D_FF + c0 + FF_CHUNK])
        act = gate * (1.0 / (1.0 + jnp.exp(-gate))) * up
        acc = acc + _dot(act.astype(BF16), wd_ref[c0:c0 + FF_CHUNK, :])
    out_ref[...] = acc


def _out_ffn_call(x2, oa2, obm2, wo, gffn, wgu, wd, layer):
    n = x2.shape[0]

    def rows(width):
        return pl.BlockSpec((TF, width), lambda i: (i, 0))

    def resident(a):
        return pl.BlockSpec((None,) + a.shape[1:], lambda i: (layer, 0, 0),
                            pipeline_mode=pl.Buffered(1))

    return pl.pallas_call(
        _out_ffn_kernel,
        grid=(n // TF,),
        in_specs=[rows(D_MODEL), rows(WIDTH_A), rows(WIDTH_B + WIDTH_M),
                  resident(wo), resident(gffn), resident(wgu), resident(wd)],
        out_specs=rows(D_MODEL),
        out_shape=jax.ShapeDtypeStruct((n, D_MODEL), F32),
        compiler_params=pltpu.CompilerParams(
            dimension_semantics=("parallel",), vmem_limit_bytes=VMEM_LIMIT),
        name="out_ffn",
    )(x2, oa2, obm2, wo, gffn, wgu, wd)


def kernel(x, mem, positions, g_mix, w_in, g_q_a, g_k_a, g_k_idx, g_q_b, g_k_b,
           rel_bias, g_q_m, g_k_m, g_mem, w_mem_kv, w_out, g_ffn, w_gate_up,
           w_down):
    B, S, D = x.shape
    depth = w_in.shape[0]

    w_in_t = jnp.swapaxes(w_in.astype(BF16), 1, 2)
    w_mem_t = jnp.swapaxes(w_mem_kv.astype(BF16), 1, 2)
    w_out_b = w_out.astype(BF16)
    w_gu_b = w_gate_up.astype(BF16)
    w_down_b = w_down.astype(BF16)
    gains = jnp.stack(
        [g_q_a, g_k_a, g_k_idx, g_q_b, g_k_b, g_q_m], axis=1)[..., None]

    inv_freq = jnp.power(
        ROPE_THETA, -jnp.arange(ROT_HALF, dtype=F32) / ROT_HALF)[:, None]
    pos3 = positions.reshape(B, 1, S)

    n_f = 2 * REL_CLIP
    fvec = jnp.concatenate(
        [rel_bias[..., 1:], jnp.broadcast_to(rel_bias[..., -1:],
                                             rel_bias.shape[:-1] + (n_f,))],
        axis=-1)[:, :, None, :]
    def amax(g):
        return jnp.max(jnp.abs(g), axis=-1)

    c = HEAD_DIM ** 0.5 * LOG2E * SHIFT_MARGIN
    bounds = jnp.stack(
        [c * amax(g_q_a) * amax(g_k_a),
         c * amax(g_q_b) * amax(g_k_b) + LOG2E * jnp.max(jnp.abs(rel_bias), (1, 2)),
         c * amax(g_q_m) * amax(g_k_m),
         jnp.zeros((depth,), F32)], axis=1)
    bias = _bias_call(bounds, fvec)

    for l in range(depth):
        qa, ka, va, qi, ki, wi, qb, kb, vb, qm = _proj_call(
            x, pos3, inv_freq, g_mix[l][None], w_in_t, l, gains[l])
        mk, mv = _memkv_call(mem, g_mem[l][None], w_mem_t, l, g_k_m[l][:, None])
        oa = _dsa_call(bounds, l, qa, ka, va, qi, ki, wi)
        obm = _band_mem_call(bounds, qb, kb, vb, bias, l, qm, mk, mv)
        x = _out_ffn_call(
            x.reshape(B * S, D), oa.reshape(B * S, WIDTH_A),
            obm.reshape(B * S, WIDTH_B + WIDTH_M), w_out_b, g_ffn[:, None],
            w_gu_b, w_down_b, l).reshape(B, S, D)
    return x
```

```python
import functools

import jax
import jax.numpy as jnp
from jax import lax
from jax.experimental import pallas as pl
from jax.experimental.pallas import tpu as pltpu

F32 = jnp.float32
BF16 = jnp.bfloat16
I32 = jnp.int32

D_MODEL = 1024
HEAD_DIM = 64
N_HEADS_A = 6
N_HEADS_B = 6
N_HEADS_M = 4
WIDTH_A = N_HEADS_A * HEAD_DIM
WIDTH_B = N_HEADS_B * HEAD_DIM
WIDTH_M = N_HEADS_M * HEAD_DIM
IDX_HEADS = 4
IDX_DIM = 64
ROT_DIM = HEAD_DIM // 4
ROT_HALF = ROT_DIM // 2
ROPE_THETA = 500000.0
CHUNK = 64
PREV_CHUNKS = 8
REL_CLIP = 256
TOPK_MAX = 256
D_FF = 2816
EPS = 1e-6
LOG2E = 1.4426950408889634

LANES = 128
SUBLANES = 8
PAIR = 2 * HEAD_DIM
V_ROWS = HEAD_DIM + 16
TM = 1024
TF = 1024
TQ = 256
KB = 256
BAND_BLOCKS = 3
WI_ROWS = 8
INT_MIN = -(2 ** 31)
KEY_BITS = 32
NEG_INF = float("-inf")
MASKED = -1e30
SHIFT_LIMIT = 40.0
SHIFT_MARGIN = 1.02
VMEM_LIMIT = 56 * 1024 * 1024

R_QA = 0
R_KA = R_QA + WIDTH_A
R_VA = R_KA + WIDTH_A
R_QI = R_VA + WIDTH_A
R_KI = R_QI + IDX_HEADS * IDX_DIM
R_WI = R_KI + IDX_DIM
R_QB = R_WI + IDX_HEADS
R_KB = R_QB + WIDTH_B
R_VB = R_KB + WIDTH_B
R_QM = R_VB + WIDTH_B
R_END = R_QM + WIDTH_M

NT_DIMS = (((1,), (1,)), ((), ()))


def _dot(a, b):
    return jnp.dot(a, b, preferred_element_type=F32)


def _dot_nt(a, b):
    return lax.dot_general(a, b, NT_DIMS, preferred_element_type=F32)


def _chunk_of(pos):
    return jnp.right_shift(pos, CHUNK.bit_length() - 1)


def _tree_sum(xs):
    while len(xs) > 1:
        xs = [a + b for a, b in zip(xs[0::2], xs[1::2])] + xs[len(xs) & ~1:]
    return xs[0]


def _as_i32(v):
    return v - (1 << 32) if v >= (1 << 31) else v


def _bit_transpose32(rows):
    rows = list(rows)
    j, m = 16, 0x0000FFFF
    while j:
        k = 0
        while k < 32:
            t = ((rows[k] ^ lax.shift_right_logical(rows[k + j], jnp.int32(j)))
                 & jnp.int32(_as_i32(m)))
            rows[k] = rows[k] ^ t
            rows[k + j] = rows[k + j] ^ lax.shift_left(t, jnp.int32(j))
            k = (k + j + 1) & ~j
        j >>= 1
        if j:
            m = (m ^ (m << j)) & 0xFFFFFFFF
    return rows


def _row_rms(x, g):
    ms = jnp.mean(x * x, axis=-1, keepdims=True)
    return x * lax.rsqrt(ms + EPS) * g


def _head_norm_t(blk, g_col):
    ms = jnp.mean(blk * blk, axis=0, keepdims=True)
    return blk * lax.rsqrt(ms + EPS) * g_col


def _rope_t(y, cos, sin):
    y1 = y[0:ROT_HALF]
    y2 = y[ROT_HALF:ROT_DIM]
    return jnp.concatenate(
        [y1 * cos - y2 * sin, y2 * cos + y1 * sin, y[ROT_DIM:]], axis=0)


def _with_ones(v_heads, n_heads):
    ones = jnp.ones((V_ROWS - HEAD_DIM, v_heads.shape[1]), v_heads.dtype)
    parts = []
    for i in range(n_heads):
        parts += [v_heads[i * HEAD_DIM:(i + 1) * HEAD_DIM], ones]
    return jnp.concatenate(parts, axis=0)


def _pair_slot(y, odd):
    z = jnp.zeros_like(y)
    return jnp.concatenate([z, y] if odd else [y, z], axis=0)


def _proj_kernel(x_ref, pos_ref, invf_ref, gmix_ref, wt_ref, gains_ref,
                 qa_ref, ka_ref, va_ref, qi_ref, ki_ref, wi_ref,
                 qb_ref, kb_ref, vb_ref, qm_ref):
    h = _row_rms(x_ref[0], gmix_ref[...]).astype(BF16)
    ang = invf_ref[...] * pos_ref[0].astype(F32)
    cos = jnp.cos(ang)
    sin = jnp.sin(ang)
    q_scale = HEAD_DIM ** -0.5 * LOG2E
    n_blk = TM // KB

    def proj_t(r0, r1):
        return _dot_nt(wt_ref[r0:r1, :], h)

    def gain(i):
        return gains_ref[i]

    def head(t, i):
        return t[i * HEAD_DIM:(i + 1) * HEAD_DIM]

    def store_q(ref, t, n_heads, g, rope):
        for i in range(n_heads):
            y = head(t, i)
            if g is not None:
                y = _head_norm_t(y, g)
            if rope:
                y = _rope_t(y, cos, sin)
            ref[0, i * PAIR:(i + 1) * PAIR, :] = _pair_slot(
                y * q_scale, i % 2 == 1).astype(BF16)

    def store_k(ref, t, n_heads, g, rope):
        for p in range(n_heads // 2):
            ys = []
            for i in (2 * p, 2 * p + 1):
                y = _head_norm_t(head(t, i), g)
                ys.append(_rope_t(y, cos, sin) if rope else y)
            blk = jnp.concatenate(ys, axis=0).T.astype(BF16)
            for j in range(n_blk):
                ref[0, j, :, p * PAIR:(p + 1) * PAIR] = blk[j * KB:(j + 1) * KB]

    def store_v(ref, t, n_heads):
        tb = _with_ones(t.astype(BF16), n_heads)
        for j in range(n_blk):
            ref[0, j] = tb[:, j * KB:(j + 1) * KB]

    store_q(qa_ref, proj_t(R_QA, R_KA), N_HEADS_A, gain(0), True)
    store_k(ka_ref, proj_t(R_KA, R_VA), N_HEADS_A, gain(1), True)
    store_v(va_ref, proj_t(R_VA, R_QI), N_HEADS_A)

    t = proj_t(R_QI, R_WI + WI_ROWS)
    for i in range(IDX_HEADS):
        y = _rope_t(head(t, i), cos, sin) * (IDX_DIM ** -0.5)
        qi_ref[0, i * PAIR:(i + 1) * PAIR, :] = _pair_slot(y, False).astype(BF16)
    ki = _rope_t(_head_norm_t(head(t, IDX_HEADS), gain(2)), cos, sin)
    ki_blk = _pair_slot(ki, False).T.astype(BF16)
    for j in range(n_blk):
        ki_ref[0, j] = ki_blk[j * KB:(j + 1) * KB]
    wi_ref[0] = t[R_WI - R_QI:R_WI - R_QI + WI_ROWS] * (IDX_HEADS ** -0.5)

    store_q(qb_ref, proj_t(R_QB, R_KB), N_HEADS_B, gain(3), False)
    store_k(kb_ref, proj_t(R_KB, R_VB), N_HEADS_B, gain(4), False)
    store_v(vb_ref, proj_t(R_VB, R_QM), N_HEADS_B)
    store_q(qm_ref, proj_t(R_QM, R_END), N_HEADS_M, gain(5), False)


def _proj_call(x, pos3, invf, gmix, wt, layer, gains):
    B, S, _ = x.shape
    nkb = S // KB
    n_blk = TM // KB

    def tok_t(rows):
        return pl.BlockSpec((1, rows, TM), lambda b, i: (b, 0, i))

    def full(shape):
        return pl.BlockSpec(shape, lambda b, i: (0,) * len(shape))

    def krow(width):
        return pl.BlockSpec((1, n_blk, KB, width), lambda b, i: (b, i, 0, 0))

    def vt(rows):
        return pl.BlockSpec((1, n_blk, rows, KB), lambda b, i: (b, i, 0, 0))

    out_shape = (
        jax.ShapeDtypeStruct((B, N_HEADS_A * PAIR, S), BF16),
        jax.ShapeDtypeStruct((B, nkb, KB, WIDTH_A), BF16),
        jax.ShapeDtypeStruct((B, nkb, N_HEADS_A * V_ROWS, KB), BF16),
        jax.ShapeDtypeStruct((B, IDX_HEADS * PAIR, S), BF16),
        jax.ShapeDtypeStruct((B, nkb, KB, PAIR), BF16),
        jax.ShapeDtypeStruct((B, WI_ROWS, S), F32),
        jax.ShapeDtypeStruct((B, N_HEADS_B * PAIR, S), BF16),
        jax.ShapeDtypeStruct((B, nkb, KB, WIDTH_B), BF16),
        jax.ShapeDtypeStruct((B, nkb, N_HEADS_B * V_ROWS, KB), BF16),
        jax.ShapeDtypeStruct((B, N_HEADS_M * PAIR, S), BF16),
    )
    out_specs = (
        tok_t(N_HEADS_A * PAIR), krow(WIDTH_A), vt(N_HEADS_A * V_ROWS),
        tok_t(IDX_HEADS * PAIR), krow(PAIR), tok_t(WI_ROWS),
        tok_t(N_HEADS_B * PAIR), krow(WIDTH_B), vt(N_HEADS_B * V_ROWS),
        tok_t(N_HEADS_M * PAIR),
    )
    return pl.pallas_call(
        _proj_kernel,
        grid=(B, S // TM),
        in_specs=[
            pl.BlockSpec((1, TM, D_MODEL), lambda b, i: (b, i, 0)),
            pl.BlockSpec((1, 1, TM), lambda b, i: (b, 0, i)),
            full((ROT_HALF, 1)),
            full((1, D_MODEL)),
            pl.BlockSpec((None, R_END, D_MODEL), lambda b, i: (layer, 0, 0)),
            full((6, HEAD_DIM, 1)),
        ],
        out_specs=out_specs,
        out_shape=out_shape,
        compiler_params=pltpu.CompilerParams(
            dimension_semantics=("parallel", "parallel"),
            vmem_limit_bytes=VMEM_LIMIT),
        name="in_proj",
    )(x, pos3, invf, gmix, wt, gains)


def _memkv_kernel(mem_ref, gmem_ref, wt_ref, gk_ref, mk_ref, mv_ref):
    hm = _row_rms(mem_ref[0], gmem_ref[...]).astype(BF16)
    kt = _dot_nt(wt_ref[0:WIDTH_M, :], hm)
    for p in range(N_HEADS_M // 2):
        ys = [_head_norm_t(kt[i * HEAD_DIM:(i + 1) * HEAD_DIM], gk_ref[...])
              for i in (2 * p, 2 * p + 1)]
        mk_ref[0, :, p * PAIR:(p + 1) * PAIR] = (
            jnp.concatenate(ys, axis=0).T.astype(BF16))
    mv_ref[0] = _with_ones(
        _dot_nt(wt_ref[WIDTH_M:2 * WIDTH_M, :], hm).astype(BF16), N_HEADS_M)


def _memkv_call(mem, gmem, wt, layer, gk):
    B, n_mem, _ = mem.shape
    return pl.pallas_call(
        _memkv_kernel,
        grid=(B,),
        in_specs=[
            pl.BlockSpec((1, n_mem, D_MODEL), lambda b: (b, 0, 0)),
            pl.BlockSpec((1, D_MODEL), lambda b: (0, 0)),
            pl.BlockSpec((None, 2 * WIDTH_M, D_MODEL), lambda b: (layer, 0, 0)),
            pl.BlockSpec((HEAD_DIM, 1), lambda b: (0, 0)),
        ],
        out_specs=(
            pl.BlockSpec((1, n_mem, WIDTH_M), lambda b: (b, 0, 0)),
            pl.BlockSpec((1, N_HEADS_M * V_ROWS, n_mem), lambda b: (b, 0, 0)),
        ),
        out_shape=(
            jax.ShapeDtypeStruct((B, n_mem, WIDTH_M), BF16),
            jax.ShapeDtypeStruct((B, N_HEADS_M * V_ROWS, n_mem), BF16),
        ),
        compiler_params=pltpu.CompilerParams(
            dimension_semantics=("parallel",), vmem_limit_bytes=VMEM_LIMIT),
        name="mem_kv",
    )(mem, gmem, wt, gk)


def _bias_kernel(bnd_ref, f_ref, out_ref):
    n_keys = BAND_BLOCKS * KB
    width = f_ref.shape[-1]
    base = jnp.broadcast_to(f_ref[0, 0], (n_keys, width))
    rolled = pltpu.roll(base, width - (n_keys - 1), 1, stride=1, stride_axis=0)
    tab = rolled[:, 0:TQ]
    key_c = _chunk_of(lax.broadcasted_iota(I32, (n_keys, TQ), 0))
    qry_c = _chunk_of(lax.broadcasted_iota(I32, (n_keys, TQ), 1)
                      + (BAND_BLOCKS - 1) * KB)
    ok = (key_c <= qry_c) & (key_c >= qry_c - PREV_CHUNKS)
    shift = bnd_ref[pl.program_id(0), 1]
    out_ref[0, 0] = jnp.where(ok, tab * LOG2E - shift, NEG_INF)


def _bias_call(bounds, fvec):
    depth, n_heads, _, width = fvec.shape
    n_keys = BAND_BLOCKS * KB
    return pl.pallas_call(
        _bias_kernel,
        grid=(depth, n_heads),
        in_specs=[pl.BlockSpec(memory_space=pltpu.SMEM),
                  pl.BlockSpec((1, 1, 1, width), lambda l, h: (l, h, 0, 0))],
        out_specs=pl.BlockSpec((1, 1, n_keys, TQ), lambda l, h: (l, h, 0, 0)),
        out_shape=jax.ShapeDtypeStruct((depth, n_heads, n_keys, TQ), F32),
        compiler_params=pltpu.CompilerParams(
            dimension_semantics=("parallel", "parallel")),
        name="band_bias",
    )(bounds, fvec)


def _col_max(x):
    return jnp.max(x, axis=0, keepdims=True)


def _col_sum(x):
    return jnp.sum(x, axis=0, keepdims=True)


def _finish_pair(outs):
    return jnp.concatenate(outs, axis=0).T.astype(BF16)


def _dsa_kernel(bnd_ref, qa_ref, ka_ref, va_ref, qi_ref, ki_ref, wi_ref, out_ref,
                keys_ref, planes_ref, mask_ref, acc_ref, *, layer):
    t = pl.program_id(1)
    shift = bnd_ref[layer, 0]
    nkb = t + 1
    n_kb = keys_ref.shape[0]
    n_pair = nkb // 2
    odd = nkb % 2 == 1
    qry_c = _chunk_of(t * TQ + lax.broadcasted_iota(I32, (KB, TQ), 1))
    key_c0 = _chunk_of(lax.broadcasted_iota(I32, (KB, TQ), 0))

    def select_topk():
        def score_blocks(kbs):
            dss = [[_dot(ki_ref[0, kb], qi_ref[0, i * PAIR:(i + 1) * PAIR, :])
                    for i in range(IDX_HEADS)] for kb in kbs]
            for kb, ds in zip(kbs, dss):
                s = jnp.zeros((KB, TQ), F32)
                for i in range(IDX_HEADS):
                    s = s + jnp.maximum(ds[i], 0.0) * wi_ref[0, i:i + 1, :]
                s = jnp.where(s == 0.0, 0.0, s)
                bits = lax.bitcast_convert_type(s, I32)
                key = jnp.where(bits >= 0, bits, bits ^ jnp.int32(0x7FFFFFFF))
                allowed = (key_c0 + kb * (KB // CHUNK)) <= qry_c
                key = jnp.where(allowed, key, jnp.int32(INT_MIN))
                keys_ref[kb] = key
                u = key ^ jnp.int32(INT_MIN)
                cols = _bit_transpose32(
                    [u[j * SUBLANES:(j + 1) * SUBLANES] for j in range(KEY_BITS)])
                for b in range(KEY_BITS):
                    planes_ref[b, kb] = cols[KEY_BITS - 1 - b]

        def score_pair(j, carry):
            score_blocks([2 * j, 2 * j + 1])
            return carry

        lax.fori_loop(0, n_pair, score_pair, 0)

        @pl.when(odd)
        def _():
            score_blocks([nkb - 1])

        def col_count(words):
            return _col_sum(_tree_sum([lax.population_count(w) for w in words]))

        def radix_select(n_blocks):
            def bit(i, carry):
                active, rank, thr_u = carry
                b = KEY_BITS - 1 - i
                planes = planes_ref[b, 0:n_blocks]
                ones = [active[kb] & planes[kb] for kb in range(n_blocks)]
                c1 = col_count(ones)
                take = c1 >= rank
                active = tuple(jnp.where(take, ones[kb], active[kb] ^ ones[kb])
                               for kb in range(n_blocks))
                rank = jnp.where(take, rank, rank - c1)
                thr_u = jnp.where(
                    take, thr_u | lax.shift_left(jnp.int32(1), b), thr_u)
                return active, rank, thr_u

            active0 = tuple(
                jnp.full((SUBLANES, TQ), jnp.where(kb < nkb, -1, 0), I32)
                for kb in range(n_blocks))
            active, rank, thr_u = lax.fori_loop(
                0, KEY_BITS, bit,
                (active0, jnp.full((1, TQ), TOPK_MAX, I32), jnp.zeros((1, TQ), I32)))
            return rank, thr_u, col_count(active)

        rank, thr_u, n_eq = lax.cond(nkb <= n_kb // 2,
                                     lambda: radix_select(n_kb // 2),
                                     lambda: radix_select(n_kb))
        thr = thr_u ^ jnp.int32(INT_MIN)
        n_ge = TOPK_MAX - rank + n_eq
        tie = (n_ge > TOPK_MAX) & (thr > INT_MIN)
        any_tie = jnp.max(jnp.where(tie, 1.0, 0.0)) > 0.0

        @pl.when(jnp.logical_not(any_tie))
        def _():
            floor = jnp.maximum(thr, jnp.int32(INT_MIN + 1))

            def blk(kb, carry):
                mask_ref[kb] = jnp.where(keys_ref[kb] >= floor, -shift, MASKED)
                return carry

            lax.fori_loop(0, nkb, blk, 0)

        @pl.when(any_tie)
        def _():
            lower = (lax.broadcasted_iota(I32, (KB, KB), 1)
                     < lax.broadcasted_iota(I32, (KB, KB), 0))
            ltri = jnp.where(lower, 1.0, 0.0).astype(BF16)
            need = rank.astype(F32)

            def blk(kb, seen):
                k = keys_ref[kb]
                eq = jnp.where(k == thr, 1.0, 0.0)
                before = _dot(ltri, eq.astype(BF16)) + seen
                keep = (k > thr) | ((k == thr) & (before < need))
                keep = keep & (k > INT_MIN)
                mask_ref[kb] = jnp.where(keep, -shift, MASKED)
                return seen + _col_sum(eq)

            lax.fori_loop(0, nkb, blk, jnp.zeros((1, TQ), F32))

    @pl.when(nkb * KB <= TOPK_MAX)
    def _():
        mask_ref[0] = jnp.where(key_c0 <= qry_c, -shift, MASKED)
        planes_ref[:, 1:] = jnp.zeros_like(planes_ref[:, 1:])

    @pl.when(nkb * KB > TOPK_MAX)
    def _():
        select_topk()

    acc_ref[...] = jnp.zeros_like(acc_ref)
    heads = range(N_HEADS_A)

    def logits(kbs):
        masks = [mask_ref[kb] for kb in kbs]
        return [[_dot(ka_ref[0, kb, :, (h // 2) * PAIR:(h // 2 + 1) * PAIR],
                      qa_ref[0, h * PAIR:(h + 1) * PAIR, :]) + mask
                 for kb, mask in zip(kbs, masks)] for h in heads]

    def weighted_values(h, kbs, es):
        return functools.reduce(jnp.add, [
            _dot(va_ref[0, kb, h * V_ROWS:(h + 1) * V_ROWS, :], e.astype(BF16))
            for kb, e in zip(kbs, es)])

    @pl.when(shift <= SHIFT_LIMIT)
    def _():
        def blocks(kbs):
            lgs = logits(kbs)
            for h in heads:
                acc_ref[h] += weighted_values(
                    h, kbs, [jnp.exp2(lg) for lg in lgs[h]])

        def pair(j, carry):
            blocks([2 * j, 2 * j + 1])
            return carry

        lax.fori_loop(0, n_pair, pair, 0)

        @pl.when(odd)
        def _():
            blocks([nkb - 1])

    @pl.when(shift > SHIFT_LIMIT)
    def _():
        def blocks(kbs, ms):
            lgs = logits(kbs)
            ms_new = []
            for h in heads:
                m_new = functools.reduce(
                    jnp.maximum, [ms[h]] + [_col_max(lg) for lg in lgs[h]])
                acc_ref[h] = (acc_ref[h] * jnp.exp2(ms[h] - m_new)
                              + weighted_values(
                                  h, kbs, [jnp.exp2(lg - m_new) for lg in lgs[h]]))
                ms_new.append(m_new)
            return tuple(ms_new)

        def pair(j, ms):
            return blocks([2 * j, 2 * j + 1], ms)

        m0 = tuple(jnp.full((1, TQ), MASKED, F32) for _ in heads)
        ms = lax.fori_loop(0, n_pair, pair, m0)

        @pl.when(odd)
        def _():
            blocks([nkb - 1], ms)

    def head_out(h):
        return acc_ref[h, 0:HEAD_DIM] / acc_ref[h, HEAD_DIM:HEAD_DIM + 1]

    for p in range(N_HEADS_A // 2):
        outs = [head_out(h) for h in (2 * p, 2 * p + 1)]
        out_ref[0, :, p * PAIR:(p + 1) * PAIR] = _finish_pair(outs)


def _dsa_call(bounds, layer, qa, ka, va, qi, ki, wi):
    B, nkb = ka.shape[0], ka.shape[1]
    S = nkb * KB

    def tile_t(rows):
        return pl.BlockSpec((1, rows, TQ), lambda b, t: (b, 0, t))

    def whole(a):
        return pl.BlockSpec((1,) + a.shape[1:], lambda b, t: (b, 0, 0, 0))

    return pl.pallas_call(
        functools.partial(_dsa_kernel, layer=layer),
        grid=(B, S // TQ),
        in_specs=[pl.BlockSpec(memory_space=pltpu.SMEM), tile_t(N_HEADS_A * PAIR), whole(ka), whole(va),
                  tile_t(IDX_HEADS * PAIR), whole(ki), tile_t(WI_ROWS)],
        out_specs=pl.BlockSpec((1, TQ, WIDTH_A), lambda b, t: (b, t, 0)),
        out_shape=jax.ShapeDtypeStruct((B, S, WIDTH_A), BF16),
        scratch_shapes=[
            pltpu.VMEM((nkb, KB, TQ), I32),
            pltpu.VMEM((KEY_BITS, nkb, SUBLANES, TQ), I32),
            pltpu.VMEM((nkb, KB, TQ), F32),
            pltpu.VMEM((N_HEADS_A, V_ROWS, TQ), F32),
        ],
        compiler_params=pltpu.CompilerParams(
            dimension_semantics=("parallel", "arbitrary"),
            vmem_limit_bytes=VMEM_LIMIT),
        name="dsa_attn",
    )(bounds, qa, ka, va, qi, ki, wi)


def _band_mem_kernel(bnd_ref, qb_ref, kb_ref, vb_ref, bias_ref, qm_ref, mk_ref,
                     mv_ref, out_ref, *, layer):
    t = pl.program_id(1)
    shift_m = bnd_ref[layer, 2]
    small = jnp.maximum(bnd_ref[layer, 1], shift_m) <= SHIFT_LIMIT
    srcs, pens = [], []
    for r in range(BAND_BLOCKS):
        blk = t - (BAND_BLOCKS - 1) + r
        srcs.append(jnp.maximum(blk, 0))
        pens.append(jnp.where(blk >= 0, 0.0, NEG_INF))

    def v_rows(h):
        return slice(h * V_ROWS, (h + 1) * V_ROWS)

    def normalised(pv):
        return pv[0:HEAD_DIM] / pv[HEAD_DIM:HEAD_DIM + 1]

    def pair_cols(h):
        return slice((h // 2) * PAIR, (h // 2 + 1) * PAIR)

    def attend(use_max):
        band_lg = [[_dot(kb_ref[0, srcs[r], :, pair_cols(h)],
                         qb_ref[0, h * PAIR:(h + 1) * PAIR, :])
                    + bias_ref[0, h, r * KB:(r + 1) * KB, :] + pens[r]
                    for r in range(BAND_BLOCKS)] for h in range(N_HEADS_B)]
        mem_lg = [_dot(mk_ref[0, :, pair_cols(h)],
                       qm_ref[0, h * PAIR:(h + 1) * PAIR, :]) - shift_m
                  for h in range(N_HEADS_M)]
        band_e = []
        for lgs in band_lg:
            if use_max:
                m = functools.reduce(jnp.maximum, [_col_max(lg) for lg in lgs])
                lgs = [lg - m for lg in lgs]
            band_e.append([jnp.exp2(lg).astype(BF16) for lg in lgs])
        mem_e = [jnp.exp2(lg - _col_max(lg) if use_max else lg).astype(BF16)
                 for lg in mem_lg]
        band_o = [normalised(functools.reduce(jnp.add, [
            _dot(vb_ref[0, srcs[r], v_rows(h), :], band_e[h][r])
            for r in range(BAND_BLOCKS)])) for h in range(N_HEADS_B)]
        mem_o = [normalised(_dot(mv_ref[0, v_rows(h), :], mem_e[h]))
                 for h in range(N_HEADS_M)]
        for p in range(N_HEADS_B // 2):
            out_ref[0, :, p * PAIR:(p + 1) * PAIR] = _finish_pair(
                band_o[2 * p:2 * p + 2])
        for p in range(N_HEADS_M // 2):
            out_ref[0, :, WIDTH_B + p * PAIR:WIDTH_B + (p + 1) * PAIR] = (
                _finish_pair(mem_o[2 * p:2 * p + 2]))

    @pl.when(small)
    def _():
        attend(False)

    @pl.when(jnp.logical_not(small))
    def _():
        attend(True)


def _band_mem_call(bounds, qb, kb, vb, bias, layer, qm, mk, mv):
    B, nkb = kb.shape[0], kb.shape[1]
    S = nkb * KB
    n_mem = mk.shape[1]

    def tile_t(rows):
        return pl.BlockSpec((1, rows, TQ), lambda b, t: (b, 0, t))

    def whole(a):
        return pl.BlockSpec((1,) + a.shape[1:], lambda b, t: (b, 0, 0, 0))

    return pl.pallas_call(
        functools.partial(_band_mem_kernel, layer=layer),
        grid=(B, S // TQ),
        in_specs=[
            pl.BlockSpec(memory_space=pltpu.SMEM),
            tile_t(N_HEADS_B * PAIR), whole(kb), whole(vb),
            pl.BlockSpec((1,) + bias.shape[1:], lambda b, t: (layer, 0, 0, 0)),
            tile_t(N_HEADS_M * PAIR),
            pl.BlockSpec((1, n_mem, WIDTH_M), lambda b, t: (b, 0, 0)),
            pl.BlockSpec((1, N_HEADS_M * V_ROWS, n_mem), lambda b, t: (b, 0, 0)),
        ],
        out_specs=pl.BlockSpec((1, TQ, WIDTH_B + WIDTH_M), lambda b, t: (b, t, 0)),
        out_shape=jax.ShapeDtypeStruct((B, S, WIDTH_B + WIDTH_M), BF16),
        compiler_params=pltpu.CompilerParams(
            dimension_semantics=("parallel", "parallel"),
            vmem_limit_bytes=VMEM_LIMIT),
        name="band_mem_attn",
    )(bounds, qb, kb, vb, bias, qm, mk, mv)


FF_CHUNK = 256


def _out_ffn_kernel(x_ref, oa_ref, obm_ref, wo_ref, gffn_ref, wgu_ref, wd_ref,
                    out_ref):
    attn = jnp.concatenate([oa_ref[...], obm_ref[...]], axis=1)
    x1 = x_ref[...] + _dot(attn, wo_ref[...])
    h = _row_rms(x1, gffn_ref[...]).astype(BF16)
    acc = x1
    for c in range(D_FF // FF_CHUNK):
        c0 = c * FF_CHUNK
        gate = _dot(h, wgu_ref[:, c0:c0 + FF_CHUNK])
        up = _dot(h, wgu_ref[:, D_FF + c0:D_FF + c0 + FF_CHUNK])
        act = gate * (1.0 / (1.0 + jnp.exp(-gate))) * up
        acc = acc + _dot(act.astype(BF16), wd_ref[c0:c0 + FF_CHUNK, :])
    out_ref[...] = acc


def _out_ffn_call(x2, oa2, obm2, wo, gffn, wgu, wd, layer):
    n = x2.shape[0]

    def rows(width):
        return pl.BlockSpec((TF, width), lambda i: (i, 0))

    def resident(a):
        return pl.BlockSpec((None,) + a.shape[1:], lambda i: (layer, 0, 0),
                            pipeline_mode=pl.Buffered(1))

    return pl.pallas_call(
        _out_ffn_kernel,
        grid=(n // TF,),
        in_specs=[rows(D_MODEL), rows(WIDTH_A), rows(WIDTH_B + WIDTH_M),
                  resident(wo), resident(gffn), resident(wgu), resident(wd)],
        out_specs=rows(D_MODEL),
        out_shape=jax.ShapeDtypeStruct((n, D_MODEL), F32),
        compiler_params=pltpu.CompilerParams(
            dimension_semantics=("parallel",), vmem_limit_bytes=VMEM_LIMIT),
        name="out_ffn",
    )(x2, oa2, obm2, wo, gffn, wgu, wd)


def kernel(x, mem, positions, g_mix, w_in, g_q_a, g_k_a, g_k_idx, g_q_b, g_k_b,
           rel_bias, g_q_m, g_k_m, g_mem, w_mem_kv, w_out, g_ffn, w_gate_up,
           w_down):
    B, S, D = x.shape
    depth = w_in.shape[0]

    w_in_t = jnp.swapaxes(w_in.astype(BF16), 1, 2)
    w_mem_t = jnp.swapaxes(w_mem_kv.astype(BF16), 1, 2)
    w_out_b = w_out.astype(BF16)
    w_gu_b = w_gate_up.astype(BF16)
    w_down_b = w_down.astype(BF16)
    gains = jnp.stack(
        [g_q_a, g_k_a, g_k_idx, g_q_b, g_k_b, g_q_m], axis=1)[..., None]

    inv_freq = jnp.power(
        ROPE_THETA, -jnp.arange(ROT_HALF, dtype=F32) / ROT_HALF)[:, None]
    pos3 = positions.reshape(B, 1, S)

    n_f = 2 * REL_CLIP
    fvec = jnp.concatenate(
        [rel_bias[..., 1:], jnp.broadcast_to(rel_bias[..., -1:],
                                             rel_bias.shape[:-1] + (n_f,))],
        axis=-1)[:, :, None, :]
    def amax(g):
        return jnp.max(jnp.abs(g), axis=-1)

    c = HEAD_DIM ** 0.5 * LOG2E * SHIFT_MARGIN
    bounds = jnp.stack(
        [c * amax(g_q_a) * amax(g_k_a),
         c * amax(g_q_b) * amax(g_k_b) + LOG2E * jnp.max(jnp.abs(rel_bias), (1, 2)),
         c * amax(g_q_m) * amax(g_k_m),
         jnp.zeros((depth,), F32)], axis=1)
    bias = _bias_call(bounds, fvec)

    for l in range(depth):
        qa, ka, va, qi, ki, wi, qb, kb, vb, qm = _proj_call(
            x, pos3, inv_freq, g_mix[l][None], w_in_t, l, gains[l])
        mk, mv = _memkv_call(mem, g_mem[l][None], w_mem_t, l, g_k_m[l][:, None])
        oa = _dsa_call(bounds, l, qa, ka, va, qi, ki, wi)
        obm = _band_mem_call(bounds, qb, kb, vb, bias, l, qm, mk, mv)
        x = _out_ffn_call(
            x.reshape(B * S, D), oa.reshape(B * S, WIDTH_A),
            obm.reshape(B * S, WIDTH_B + WIDTH_M), w_out_b, g_ffn[:, None],
            w_gu_b, w_down_b, l).reshape(B, S, D)
    return x
```

```python
import functools

import jax
import jax.numpy as jnp
from jax import lax
from jax.experimental import pallas as pl
from jax.experimental.pallas import tpu as pltpu

F32 = jnp.float32
BF16 = jnp.bfloat16
I32 = jnp.int32

D_MODEL = 1024
HEAD_DIM = 64
N_HEADS_A = 6
N_HEADS_B = 6
N_HEADS_M = 4
WIDTH_A = N_HEADS_A * HEAD_DIM
WIDTH_B = N_HEADS_B * HEAD_DIM
WIDTH_M = N_HEADS_M * HEAD_DIM
IDX_HEADS = 4
IDX_DIM = 64
ROT_DIM = HEAD_DIM // 4
ROT_HALF = ROT_DIM // 2
ROPE_THETA = 500000.0
CHUNK = 64
PREV_CHUNKS = 8
REL_CLIP = 256
TOPK_MAX = 256
D_FF = 2816
EPS = 1e-6
LOG2E = 1.4426950408889634

LANES = 128
SUBLANES = 8
PAIR = 2 * HEAD_DIM
V_ROWS = HEAD_DIM + 16
TM = 1024
TF = 1024
TQ = 256
KB = 256
BAND_BLOCKS = 3
WI_ROWS = 8
INT_MIN = -(2 ** 31)
KEY_BITS = 32
NEG_INF = float("-inf")
MASKED = -1e30
SHIFT_LIMIT = 40.0
SHIFT_MARGIN = 1.02
VMEM_LIMIT = 56 * 1024 * 1024

R_QA = 0
R_KA = R_QA + WIDTH_A
R_VA = R_KA + WIDTH_A
R_QI = R_VA + WIDTH_A
R_KI = R_QI + IDX_HEADS * IDX_DIM
R_WI = R_KI + IDX_DIM
R_QB = R_WI + IDX_HEADS
R_KB = R_QB + WIDTH_B
R_VB = R_KB + WIDTH_B
R_QM = R_VB + WIDTH_B
R_END = R_QM + WIDTH_M

NT_DIMS = (((1,), (1,)), ((), ()))


def _dot(a, b):
    return jnp.dot(a, b, preferred_element_type=F32)


def _dot_nt(a, b):
    return lax.dot_general(a, b, NT_DIMS, preferred_element_type=F32)


def _chunk_of(pos):
    return jnp.right_shift(pos, CHUNK.bit_length() - 1)


def _tree_sum(xs):
    while len(xs) > 1:
        xs = [a + b for a, b in zip(xs[0::2], xs[1::2])] + xs[len(xs) & ~1:]
    return xs[0]


def _as_i32(v):
    return v - (1 << 32) if v >= (1 << 31) else v


def _bit_transpose32(rows):
    rows = list(rows)
    j, m = 16, 0x0000FFFF
    while j:
        k = 0
        while k < 32:
            t = ((rows[k] ^ lax.shift_right_logical(rows[k + j], jnp.int32(j)))
                 & jnp.int32(_as_i32(m)))
            rows[k] = rows[k] ^ t
            rows[k + j] = rows[k + j] ^ lax.shift_left(t, jnp.int32(j))
            k = (k + j + 1) & ~j
        j >>= 1
        if j:
            m = (m ^ (m << j)) & 0xFFFFFFFF
    return rows


def _row_rms(x, g):
    ms = jnp.mean(x * x, axis=-1, keepdims=True)
    return x * lax.rsqrt(ms + EPS) * g


def _head_norm_t(blk, g_col):
    ms = jnp.mean(blk * blk, axis=0, keepdims=True)
    return blk * lax.rsqrt(ms + EPS) * g_col


def _rope_t(y, cos, sin):
    y1 = y[0:ROT_HALF]
    y2 = y[ROT_HALF:ROT_DIM]
    return jnp.concatenate(
        [y1 * cos - y2 * sin, y2 * cos + y1 * sin, y[ROT_DIM:]], axis=0)


def _with_ones(v_heads, n_heads):
    ones = jnp.ones((V_ROWS - HEAD_DIM, v_heads.shape[1]), v_heads.dtype)
    parts = []
    for i in range(n_heads):
        parts += [v_heads[i * HEAD_DIM:(i + 1) * HEAD_DIM], ones]
    return jnp.concatenate(parts, axis=0)


def _pair_slot(y, odd):
    z = jnp.zeros_like(y)
    return jnp.concatenate([z, y] if odd else [y, z], axis=0)


def _proj_kernel(x_ref, pos_ref, invf_ref, gmix_ref, wt_ref, gains_ref,
                 qa_ref, ka_ref, va_ref, qi_ref, ki_ref, wi_ref,
                 qb_ref, kb_ref, vb_ref, qm_ref):
    h = _row_rms(x_ref[0], gmix_ref[...]).astype(BF16)
    ang = invf_ref[...] * pos_ref[0].astype(F32)
    cos = jnp.cos(ang)
    sin = jnp.sin(ang)
    q_scale = HEAD_DIM ** -0.5 * LOG2E
    n_blk = TM // KB

    def proj_t(r0, r1):
        return _dot_nt(wt_ref[r0:r1, :], h)

    def gain(i):
        return gains_ref[i]

    def head(t, i):
        return t[i * HEAD_DIM:(i + 1) * HEAD_DIM]

    def store_q(ref, t, n_heads, g, rope):
        for i in range(n_heads):
            y = head(t, i)
            if g is not None:
                y = _head_norm_t(y, g)
            if rope:
                y = _rope_t(y, cos, sin)
            ref[0, i * PAIR:(i + 1) * PAIR, :] = _pair_slot(
                y * q_scale, i % 2 == 1).astype(BF16)

    def store_k(ref, t, n_heads, g, rope):
        for p in range(n_heads // 2):
            ys = []
            for i in (2 * p, 2 * p + 1):
                y = _head_norm_t(head(t, i), g)
                ys.append(_rope_t(y, cos, sin) if rope else y)
            blk = jnp.concatenate(ys, axis=0).T.astype(BF16)
            for j in range(n_blk):
                ref[0, j, :, p * PAIR:(p + 1) * PAIR] = blk[j * KB:(j + 1) * KB]

    def store_v(ref, t, n_heads):
        tb = _with_ones(t.astype(BF16), n_heads)
        for j in range(n_blk):
            ref[0, j] = tb[:, j * KB:(j + 1) * KB]

    store_q(qa_ref, proj_t(R_QA, R_KA), N_HEADS_A, gain(0), True)
    store_k(ka_ref, proj_t(R_KA, R_VA), N_HEADS_A, gain(1), True)
    store_v(va_ref, proj_t(R_VA, R_QI), N_HEADS_A)

    t = proj_t(R_QI, R_WI + WI_ROWS)
    for i in range(IDX_HEADS):
        y = _rope_t(head(t, i), cos, sin) * (IDX_DIM ** -0.5)
        qi_ref[0, i * PAIR:(i + 1) * PAIR, :] = _pair_slot(y, False).astype(BF16)
    ki = _rope_t(_head_norm_t(head(t, IDX_HEADS), gain(2)), cos, sin)
    ki_blk = _pair_slot(ki, False).T.astype(BF16)
    for j in range(n_blk):
        ki_ref[0, j] = ki_blk[j * KB:(j + 1) * KB]
    wi_ref[0] = t[R_WI - R_QI:R_WI - R_QI + WI_ROWS] * (IDX_HEADS ** -0.5)

    store_q(qb_ref, proj_t(R_QB, R_KB), N_HEADS_B, gain(3), False)
    store_k(kb_ref, proj_t(R_KB, R_VB), N_HEADS_B, gain(4), False)
    store_v(vb_ref, proj_t(R_VB, R_QM), N_HEADS_B)
    store_q(qm_ref, proj_t(R_QM, R_END), N_HEADS_M, gain(5), False)


def _proj_call(x, pos3, invf, gmix, wt, layer, gains):
    B, S, _ = x.shape
    nkb = S // KB
    n_blk = TM // KB

    def tok_t(rows):
        return pl.BlockSpec((1, rows, TM), lambda b, i: (b, 0, i))

    def full(shape):
        return pl.BlockSpec(shape, lambda b, i: (0,) * len(shape))

    def krow(width):
        return pl.BlockSpec((1, n_blk, KB, width), lambda b, i: (b, i, 0, 0))

    def vt(rows):
        return pl.BlockSpec((1, n_blk, rows, KB), lambda b, i: (b, i, 0, 0))

    out_shape = (
        jax.ShapeDtypeStruct((B, N_HEADS_A * PAIR, S), BF16),
        jax.ShapeDtypeStruct((B, nkb, KB, WIDTH_A), BF16),
        jax.ShapeDtypeStruct((B, nkb, N_HEADS_A * V_ROWS, KB), BF16),
        jax.ShapeDtypeStruct((B, IDX_HEADS * PAIR, S), BF16),
        jax.ShapeDtypeStruct((B, nkb, KB, PAIR), BF16),
        jax.ShapeDtypeStruct((B, WI_ROWS, S), F32),
        jax.ShapeDtypeStruct((B, N_HEADS_B * PAIR, S), BF16),
        jax.ShapeDtypeStruct((B, nkb, KB, WIDTH_B), BF16),
        jax.ShapeDtypeStruct((B, nkb, N_HEADS_B * V_ROWS, KB), BF16),
        jax.ShapeDtypeStruct((B, N_HEADS_M * PAIR, S), BF16),
    )
    out_specs = (
        tok_t(N_HEADS_A * PAIR), krow(WIDTH_A), vt(N_HEADS_A * V_ROWS),
        tok_t(IDX_HEADS * PAIR), krow(PAIR), tok_t(WI_ROWS),
        tok_t(N_HEADS_B * PAIR), krow(WIDTH_B), vt(N_HEADS_B * V_ROWS),
        tok_t(N_HEADS_M * PAIR),
    )
    return pl.pallas_call(
        _proj_kernel,
        grid=(B, S // TM),
        in_specs=[
            pl.BlockSpec((1, TM, D_MODEL), lambda b, i: (b, i, 0)),
            pl.BlockSpec((1, 1, TM), lambda b, i: (b, 0, i)),
            full((ROT_HALF, 1)),
            full((1, D_MODEL)),
            pl.BlockSpec((None, R_END, D_MODEL), lambda b, i: (layer, 0, 0)),
            full((6, HEAD_DIM, 1)),
        ],
        out_specs=out_specs,
        out_shape=out_shape,
        compiler_params=pltpu.CompilerParams(
            dimension_semantics=("parallel", "parallel"),
            vmem_limit_bytes=VMEM_LIMIT),
        name="in_proj",
    )(x, pos3, invf, gmix, wt, gains)


def _memkv_kernel(mem_ref, gmem_ref, wt_ref, gk_ref, mk_ref, mv_ref):
    hm = _row_rms(mem_ref[0], gmem_ref[...]).astype(BF16)
    kt = _dot_nt(wt_ref[0:WIDTH_M, :], hm)
    for p in range(N_HEADS_M // 2):
        ys = [_head_norm_t(kt[i * HEAD_DIM:(i + 1) * HEAD_DIM], gk_ref[...])
              for i in (2 * p, 2 * p + 1)]
        mk_ref[0, :, p * PAIR:(p + 1) * PAIR] = (
            jnp.concatenate(ys, axis=0).T.astype(BF16))
    mv_ref[0] = _with_ones(
        _dot_nt(wt_ref[WIDTH_M:2 * WIDTH_M, :], hm).astype(BF16), N_HEADS_M)


def _memkv_call(mem, gmem, wt, layer, gk):
    B, n_mem, _ = mem.shape
    return pl.pallas_call(
        _memkv_kernel,
        grid=(B,),
        in_specs=[
            pl.BlockSpec((1, n_mem, D_MODEL), lambda b: (b, 0, 0)),
            pl.BlockSpec((1, D_MODEL), lambda b: (0, 0)),
            pl.BlockSpec((None, 2 * WIDTH_M, D_MODEL), lambda b: (layer, 0, 0)),
            pl.BlockSpec((HEAD_DIM, 1), lambda b: (0, 0)),
        ],
        out_specs=(
            pl.BlockSpec((1, n_mem, WIDTH_M), lambda b: (b, 0, 0)),
            pl.BlockSpec((1, N_HEADS_M * V_ROWS, n_mem), lambda b: (b, 0, 0)),
        ),
        out_shape=(
            jax.ShapeDtypeStruct((B, n_mem, WIDTH_M), BF16),
            jax.ShapeDtypeStruct((B, N_HEADS_M * V_ROWS, n_mem), BF16),
        ),
        compiler_params=pltpu.CompilerParams(
            dimension_semantics=("parallel",), vmem_limit_bytes=VMEM_LIMIT),
        name="mem_kv",
    )(mem, gmem, wt, gk)


def _bias_kernel(bnd_ref, f_ref, out_ref):
    n_keys = BAND_BLOCKS * KB
    width = f_ref.shape[-1]
    base = jnp.broadcast_to(f_ref[0, 0], (n_keys, width))
    rolled = pltpu.roll(base, width - (n_keys - 1), 1, stride=1, stride_axis=0)
    tab = rolled[:, 0:TQ]
    key_c = _chunk_of(lax.broadcasted_iota(I32, (n_keys, TQ), 0))
    qry_c = _chunk_of(lax.broadcasted_iota(I32, (n_keys, TQ), 1)
                      + (BAND_BLOCKS - 1) * KB)
    ok = (key_c <= qry_c) & (key_c >= qry_c - PREV_CHUNKS)
    shift = bnd_ref[pl.program_id(0), 1]
    out_ref[0, 0] = jnp.where(ok, tab * LOG2E - shift, NEG_INF)


def _bias_call(bounds, fvec):
    depth, n_heads, _, width = fvec.shape
    n_keys = BAND_BLOCKS * KB
    return pl.pallas_call(
        _bias_kernel,
        grid=(depth, n_heads),
        in_specs=[pl.BlockSpec(memory_space=pltpu.SMEM),
                  pl.BlockSpec((1, 1, 1, width), lambda l, h: (l, h, 0, 0))],
        out_specs=pl.BlockSpec((1, 1, n_keys, TQ), lambda l, h: (l, h, 0, 0)),
        out_shape=jax.ShapeDtypeStruct((depth, n_heads, n_keys, TQ), F32),
        compiler_params=pltpu.CompilerParams(
            dimension_semantics=("parallel", "parallel")),
        name="band_bias",
    )(bounds, fvec)


def _col_max(x):
    return jnp.max(x, axis=0, keepdims=True)


def _col_sum(x):
    return jnp.sum(x, axis=0, keepdims=True)


def _finish_pair(outs):
    return jnp.concatenate(outs, axis=0).T.astype(BF16)


def _dsa_kernel(bnd_ref, qa_ref, ka_ref, va_ref, qi_ref, ki_ref, wi_ref, out_ref,
                keys_ref, planes_ref, mask_ref, acc_ref, *, layer):
    t = pl.program_id(1)
    shift = bnd_ref[layer, 0]
    nkb = t + 1
    n_kb = keys_ref.shape[0]
    n_pair = nkb // 2
    odd = nkb % 2 == 1
    qry_c = _chunk_of(t * TQ + lax.broadcasted_iota(I32, (KB, TQ), 1))
    key_c0 = _chunk_of(lax.broadcasted_iota(I32, (KB, TQ), 0))

    def select_topk():
        def score_blocks(kbs):
            dss = [[_dot(ki_ref[0, kb], qi_ref[0, i * PAIR:(i + 1) * PAIR, :])
                    for i in range(IDX_HEADS)] for kb in kbs]
            for kb, ds in zip(kbs, dss):
                s = jnp.zeros((KB, TQ), F32)
                for i in range(IDX_HEADS):
                    s = s + jnp.maximum(ds[i], 0.0) * wi_ref[0, i:i + 1, :]
                s = jnp.where(s == 0.0, 0.0, s)
                bits = lax.bitcast_convert_type(s, I32)
                key = jnp.where(bits >= 0, bits, bits ^ jnp.int32(0x7FFFFFFF))
                allowed = (key_c0 + kb * (KB // CHUNK)) <= qry_c
                key = jnp.where(allowed, key, jnp.int32(INT_MIN))
                keys_ref[kb] = key
                u = key ^ jnp.int32(INT_MIN)
                cols = _bit_transpose32(
                    [u[j * SUBLANES:(j + 1) * SUBLANES] for j in range(KEY_BITS)])
                for b in range(KEY_BITS):
                    planes_ref[b, kb] = cols[KEY_BITS - 1 - b]

        def score_pair(j, carry):
            score_blocks([2 * j, 2 * j + 1])
            return carry

        lax.fori_loop(0, n_pair, score_pair, 0)

        @pl.when(odd)
        def _():
            score_blocks([nkb - 1])

        def col_count(words):
            return _col_sum(_tree_sum([lax.population_count(w) for w in words]))

        def radix_select(n_blocks):
            def bit(i, carry):
                active, rank, thr_u = carry
                b = KEY_BITS - 1 - i
                planes = planes_ref[b, 0:n_blocks]
                ones = [active[kb] & planes[kb] for kb in range(n_blocks)]
                c1 = col_count(ones)
                take = c1 >= rank
                active = tuple(jnp.where(take, ones[kb], active[kb] ^ ones[kb])
                               for kb in range(n_blocks))
                rank = jnp.where(take, rank, rank - c1)
                thr_u = jnp.where(
                    take, thr_u | lax.shift_left(jnp.int32(1), b), thr_u)
                return active, rank, thr_u

            active0 = tuple(
                jnp.full((SUBLANES, TQ), jnp.where(kb < nkb, -1, 0), I32)
                for kb in range(n_blocks))
            active, rank, thr_u = lax.fori_loop(
                0, KEY_BITS, bit,
                (active0, jnp.full((1, TQ), TOPK_MAX, I32), jnp.zeros((1, TQ), I32)))
            return rank, thr_u, col_count(active)

        rank, thr_u, n_eq = lax.cond(nkb <= n_kb // 2,
                                     lambda: radix_select(n_kb // 2),
                                     lambda: radix_select(n_kb))
        thr = thr_u ^ jnp.int32(INT_MIN)
        n_ge = TOPK_MAX - rank + n_eq
        tie = (n_ge > TOPK_MAX) & (thr > INT_MIN)
        any_tie = jnp.max(jnp.where(tie, 1.0, 0.0)) > 0.0

        @pl.when(jnp.logical_not(any_tie))
        def _():
            floor = jnp.maximum(thr, jnp.int32(INT_MIN + 1))

            def blk(kb, carry):
                mask_ref[kb] = jnp.where(keys_ref[kb] >= floor, -shift, MASKED)
                return carry

            lax.fori_loop(0, nkb, blk, 0)

        @pl.when(any_tie)
        def _():
            lower = (lax.broadcasted_iota(I32, (KB, KB), 1)
                     < lax.broadcasted_iota(I32, (KB, KB), 0))
            ltri = jnp.where(lower, 1.0, 0.0).astype(BF16)
            need = rank.astype(F32)

            def blk(kb, seen):
                k = keys_ref[kb]
                eq = jnp.where(k == thr, 1.0, 0.0)
                before = _dot(ltri, eq.astype(BF16)) + seen
                keep = (k > thr) | ((k == thr) & (before < need))
                keep = keep & (k > INT_MIN)
                mask_ref[kb] = jnp.where(keep, -shift, MASKED)
                return seen + _col_sum(eq)

            lax.fori_loop(0, nkb, blk, jnp.zeros((1, TQ), F32))

    @pl.when(nkb * KB <= TOPK_MAX)
    def _():
        mask_ref[0] = jnp.where(key_c0 <= qry_c, -shift, MASKED)
        planes_ref[:, 1:] = jnp.zeros_like(planes_ref[:, 1:])

    @pl.when(nkb * KB > TOPK_MAX)
    def _():
        select_topk()

    acc_ref[...] = jnp.zeros_like(acc_ref)
    heads = range(N_HEADS_A)

    def logits(kbs):
        masks = [mask_ref[kb] for kb in kbs]
        return [[_dot(ka_ref[0, kb, :, (h // 2) * PAIR:(h // 2 + 1) * PAIR],
                      qa_ref[0, h * PAIR:(h + 1) * PAIR, :]) + mask
                 for kb, mask in zip(kbs, masks)] for h in heads]

    def weighted_values(h, kbs, es):
        return functools.reduce(jnp.add, [
            _dot(va_ref[0, kb, h * V_ROWS:(h + 1) * V_ROWS, :], e.astype(BF16))
            for kb, e in zip(kbs, es)])

    @pl.when(shift <= SHIFT_LIMIT)
    def _():
        def blocks(kbs):
            lgs = logits(kbs)
            for h in heads:
                acc_ref[h] += weighted_values(
                    h, kbs, [jnp.exp2(lg) for lg in lgs[h]])

        def triple(j, carry):
            blocks([3 * j, 3 * j + 1, 3 * j + 2])
            return carry

        lax.fori_loop(0, nkb // 3, triple, 0)

        @pl.when(nkb % 3 == 1)
        def _():
            blocks([nkb - 1])

        @pl.when(nkb % 3 == 2)
        def _():
            blocks([nkb - 2, nkb - 1])

    @pl.when(shift > SHIFT_LIMIT)
    def _():
        def blocks(kbs, ms):
            lgs = logits(kbs)
            ms_new = []
            for h in heads:
                m_new = functools.reduce(
                    jnp.maximum, [ms[h]] + [_col_max(lg) for lg in lgs[h]])
                acc_ref[h] = (acc_ref[h] * jnp.exp2(ms[h] - m_new)
                              + weighted_values(
                                  h, kbs, [jnp.exp2(lg - m_new) for lg in lgs[h]]))
                ms_new.append(m_new)
            return tuple(ms_new)

        def pair(j, ms):
            return blocks([2 * j, 2 * j + 1], ms)

        m0 = tuple(jnp.full((1, TQ), MASKED, F32) for _ in heads)
        ms = lax.fori_loop(0, n_pair, pair, m0)

        @pl.when(odd)
        def _():
            blocks([nkb - 1], ms)

    def head_out(h):
        return acc_ref[h, 0:HEAD_DIM] / acc_ref[h, HEAD_DIM:HEAD_DIM + 1]

    for p in range(N_HEADS_A // 2):
        outs = [head_out(h) for h in (2 * p, 2 * p + 1)]
        out_ref[0, :, p * PAIR:(p + 1) * PAIR] = _finish_pair(outs)


def _dsa_call(bounds, layer, qa, ka, va, qi, ki, wi):
    B, nkb = ka.shape[0], ka.shape[1]
    S = nkb * KB

    def tile_t(rows):
        return pl.BlockSpec((1, rows, TQ), lambda b, t: (b, 0, t))

    def whole(a):
        return pl.BlockSpec((1,) + a.shape[1:], lambda b, t: (b, 0, 0, 0))

    return pl.pallas_call(
        functools.partial(_dsa_kernel, layer=layer),
        grid=(B, S // TQ),
        in_specs=[pl.BlockSpec(memory_space=pltpu.SMEM), tile_t(N_HEADS_A * PAIR), whole(ka), whole(va),
                  tile_t(IDX_HEADS * PAIR), whole(ki), tile_t(WI_ROWS)],
        out_specs=pl.BlockSpec((1, TQ, WIDTH_A), lambda b, t: (b, t, 0)),
        out_shape=jax.ShapeDtypeStruct((B, S, WIDTH_A), BF16),
        scratch_shapes=[
            pltpu.VMEM((nkb, KB, TQ), I32),
            pltpu.VMEM((KEY_BITS, nkb, SUBLANES, TQ), I32),
            pltpu.VMEM((nkb, KB, TQ), F32),
            pltpu.VMEM((N_HEADS_A, V_ROWS, TQ), F32),
        ],
        compiler_params=pltpu.CompilerParams(
            dimension_semantics=("parallel", "arbitrary"),
            vmem_limit_bytes=VMEM_LIMIT),
        name="dsa_attn",
    )(bounds, qa, ka, va, qi, ki, wi)


def _band_mem_kernel(bnd_ref, qb_ref, kb_ref, vb_ref, bias_ref, qm_ref, mk_ref,
                     mv_ref, out_ref, *, layer):
    t = pl.program_id(1)
    shift_m = bnd_ref[layer, 2]
    small = jnp.maximum(bnd_ref[layer, 1], shift_m) <= SHIFT_LIMIT
    srcs, pens = [], []
    for r in range(BAND_BLOCKS):
        blk = t - (BAND_BLOCKS - 1) + r
        srcs.append(jnp.maximum(blk, 0))
        pens.append(jnp.where(blk >= 0, 0.0, NEG_INF))

    def v_rows(h):
        return slice(h * V_ROWS, (h + 1) * V_ROWS)

    def normalised(pv):
        return pv[0:HEAD_DIM] / pv[HEAD_DIM:HEAD_DIM + 1]

    def pair_cols(h):
        return slice((h // 2) * PAIR, (h // 2 + 1) * PAIR)

    def attend(use_max):
        band_lg = [[_dot(kb_ref[0, srcs[r], :, pair_cols(h)],
                         qb_ref[0, h * PAIR:(h + 1) * PAIR, :])
                    + bias_ref[0, h, r * KB:(r + 1) * KB, :] + pens[r]
                    for r in range(BAND_BLOCKS)] for h in range(N_HEADS_B)]
        mem_lg = [_dot(mk_ref[0, :, pair_cols(h)],
                       qm_ref[0, h * PAIR:(h + 1) * PAIR, :]) - shift_m
                  for h in range(N_HEADS_M)]
        band_e = []
        for lgs in band_lg:
            if use_max:
                m = functools.reduce(jnp.maximum, [_col_max(lg) for lg in lgs])
                lgs = [lg - m for lg in lgs]
            band_e.append([jnp.exp2(lg).astype(BF16) for lg in lgs])
        mem_e = [jnp.exp2(lg - _col_max(lg) if use_max else lg).astype(BF16)
                 for lg in mem_lg]
        band_o = [normalised(functools.reduce(jnp.add, [
            _dot(vb_ref[0, srcs[r], v_rows(h), :], band_e[h][r])
            for r in range(BAND_BLOCKS)])) for h in range(N_HEADS_B)]
        mem_o = [normalised(_dot(mv_ref[0, v_rows(h), :], mem_e[h]))
                 for h in range(N_HEADS_M)]
        for p in range(N_HEADS_B // 2):
            out_ref[0, :, p * PAIR:(p + 1) * PAIR] = _finish_pair(
                band_o[2 * p:2 * p + 2])
        for p in range(N_HEADS_M // 2):
            out_ref[0, :, WIDTH_B + p * PAIR:WIDTH_B + (p + 1) * PAIR] = (
                _finish_pair(mem_o[2 * p:2 * p + 2]))

    @pl.when(small)
    def _():
        attend(False)

    @pl.when(jnp.logical_not(small))
    def _():
        attend(True)


def _band_mem_call(bounds, qb, kb, vb, bias, layer, qm, mk, mv):
    B, nkb = kb.shape[0], kb.shape[1]
    S = nkb * KB
    n_mem = mk.shape[1]

    def tile_t(rows):
        return pl.BlockSpec((1, rows, TQ), lambda b, t: (b, 0, t))

    def whole(a):
        return pl.BlockSpec((1,) + a.shape[1:], lambda b, t: (b, 0, 0, 0))

    return pl.pallas_call(
        functools.partial(_band_mem_kernel, layer=layer),
        grid=(B, S // TQ),
        in_specs=[
            pl.BlockSpec(memory_space=pltpu.SMEM),
            tile_t(N_HEADS_B * PAIR), whole(kb), whole(vb),
            pl.BlockSpec((1,) + bias.shape[1:], lambda b, t: (layer, 0, 0, 0)),
            tile_t(N_HEADS_M * PAIR),
            pl.BlockSpec((1, n_mem, WIDTH_M), lambda b, t: (b, 0, 0)),
            pl.BlockSpec((1, N_HEADS_M * V_ROWS, n_mem), lambda b, t: (b, 0, 0)),
        ],
        out_specs=pl.BlockSpec((1, TQ, WIDTH_B + WIDTH_M), lambda b, t: (b, t, 0)),
        out_shape=jax.ShapeDtypeStruct((B, S, WIDTH_B + WIDTH_M), BF16),
        compiler_params=pltpu.CompilerParams(
            dimension_semantics=("parallel", "parallel"),
            vmem_limit_bytes=VMEM_LIMIT),
        name="band_mem_attn",
    )(bounds, qb, kb, vb, bias, qm, mk, mv)


FF_CHUNK = 256


def _out_ffn_kernel(x_ref, oa_ref, obm_ref, wo_ref, gffn_ref, wgu_ref, wd_ref,
                    out_ref):
    attn = jnp.concatenate([oa_ref[...], obm_ref[...]], axis=1)
    x1 = x_ref[...] + _dot(attn, wo_ref[...])
    h = _row_rms(x1, gffn_ref[...]).astype(BF16)
    acc = x1
    for c in range(D_FF // FF_CHUNK):
        c0 = c * FF_CHUNK
        gate = _dot(h, wgu_ref[:, c0:c0 + FF_CHUNK])
        up = _dot(h, wgu_ref[:, D_FF + c0:D_FF + c0 + FF_CHUNK])
        act = gate * (1.0 / (1.0 + jnp.exp(-gate))) * up
        acc = acc + _dot(act.astype(BF16), wd_ref[c0:c0 + FF_CHUNK, :])
    out_ref[...] = acc


def _out_ffn_call(x2, oa2, obm2, wo, gffn, wgu, wd, layer):
    n = x2.shape[0]

    def rows(width):
        return pl.BlockSpec((TF, width), lambda i: (i, 0))

    def resident(a):
        return pl.BlockSpec((None,) + a.shape[1:], lambda i: (layer, 0, 0),
                            pipeline_mode=pl.Buffered(1))

    return pl.pallas_call(
        _out_ffn_kernel,
        grid=(n // TF,),
        in_specs=[rows(D_MODEL), rows(WIDTH_A), rows(WIDTH_B + WIDTH_M),
                  resident(wo), resident(gffn), resident(wgu), resident(wd)],
        out_specs=rows(D_MODEL),
        out_shape=jax.ShapeDtypeStruct((n, D_MODEL), F32),
        compiler_params=pltpu.CompilerParams(
            dimension_semantics=("parallel",), vmem_limit_bytes=VMEM_LIMIT),
        name="out_ffn",
    )(x2, oa2, obm2, wo, gffn, wgu, wd)


def kernel(x, mem, positions, g_mix, w_in, g_q_a, g_k_a, g_k_idx, g_q_b, g_k_b,
           rel_bias, g_q_m, g_k_m, g_mem, w_mem_kv, w_out, g_ffn, w_gate_up,
           w_down):
    B, S, D = x.shape
    depth = w_in.shape[0]

    w_in_t = jnp.swapaxes(w_in.astype(BF16), 1, 2)
    w_mem_t = jnp.swapaxes(w_mem_kv.astype(BF16), 1, 2)
    w_out_b = w_out.astype(BF16)
    w_gu_b = w_gate_up.astype(BF16)
    w_down_b = w_down.astype(BF16)
    gains = jnp.stack(
        [g_q_a, g_k_a, g_k_idx, g_q_b, g_k_b, g_q_m], axis=1)[..., None]

    inv_freq = jnp.power(
        ROPE_THETA, -jnp.arange(ROT_HALF, dtype=F32) / ROT_HALF)[:, None]
    pos3 = positions.reshape(B, 1, S)

    n_f = 2 * REL_CLIP
    fvec = jnp.concatenate(
        [rel_bias[..., 1:], jnp.broadcast_to(rel_bias[..., -1:],
                                             rel_bias.shape[:-1] + (n_f,))],
        axis=-1)[:, :, None, :]
    def amax(g):
        return jnp.max(jnp.abs(g), axis=-1)

    c = HEAD_DIM ** 0.5 * LOG2E * SHIFT_MARGIN
    bounds = jnp.stack(
        [c * amax(g_q_a) * amax(g_k_a),
         c * amax(g_q_b) * amax(g_k_b) + LOG2E * jnp.max(jnp.abs(rel_bias), (1, 2)),
         c * amax(g_q_m) * amax(g_k_m),
         jnp.zeros((depth,), F32)], axis=1)
    bias = _bias_call(bounds, fvec)

    for l in range(depth):
        qa, ka, va, qi, ki, wi, qb, kb, vb, qm = _proj_call(
            x, pos3, inv_freq, g_mix[l][None], w_in_t, l, gains[l])
        mk, mv = _memkv_call(mem, g_mem[l][None], w_mem_t, l, g_k_m[l][:, None])
        oa = _dsa_call(bounds, l, qa, ka, va, qi, ki, wi)
        obm = _band_mem_call(bounds, qb, kb, vb, bias, l, qm, mk, mv)
        x = _out_ffn_call(
            x.reshape(B * S, D), oa.reshape(B * S, WIDTH_A),
            obm.reshape(B * S, WIDTH_B + WIDTH_M), w_out_b, g_ffn[:, None],
            w_gu_b, w_down_b, l).reshape(B, S, D)
    return x
```
